```python
import math
import jax
import jax.numpy as jnp
from jax import lax
import numpy as np

D_MODEL = 1024
BATCH = 16
SEQ = 4096
DEPTH = 4

N_MIXERS = 4
NORM_EPS = 1e-6
Q_BLOCK = 128
NEG_INF = -1e30

GDN_HEADS = 8
GDN_HEAD_K = 128
GDN_HEAD_V = 128
GDN_CONV = 4
GDN_CHUNK = 64
GDN_KEY_DIM = GDN_HEADS * GDN_HEAD_K
GDN_VAL_DIM = GDN_HEADS * GDN_HEAD_V

SB_HEADS = 16
SB_HEAD_DIM = 64

SWA_Q_HEADS = 16
SWA_KV_HEADS = 2
SWA_HEAD_DIM = 64
SWA_WINDOW = 128

REL_BUCKETS = 32
REL_MAX_DIST = 128

MLA_HEADS = 8
MLA_Q_RANK = 384
MLA_KV_RANK = 256
MLA_NOPE = 128
MLA_ROPE = 64
MLA_V = 128
ROPE_THETA = 10000.0

FFN_DENSE = 2816
N_EXPERTS = 8
TOP_K = 2
FFN_EXPERT = 3584

N_GDN = (DEPTH + 3) // 4
N_SB = (DEPTH + 2) // 4
N_SWA = (DEPTH + 1) // 4
N_MLA = DEPTH // 4
N_DENSE = (DEPTH + 1) // 2
N_MOE = DEPTH // 2

kernel_name = "hybrid_gdn_stickbreak_swa_mla_moe_trunk"


def rms_norm(x, w):
    x32 = x.astype(jnp.float32)
    y = x32 * lax.rsqrt(jnp.mean(x32 * x32, axis=-1, keepdims=True) + NORM_EPS)
    return (y * w.astype(jnp.float32)).astype(x.dtype)


def l2_normalize(x):
    return x * lax.rsqrt(jnp.sum(x * x, axis=-1, keepdims=True) + NORM_EPS)


def causal_depthwise_conv(x, w):
    k, c = w.shape
    return lax.conv_general_dilated(
        x, w[:, None, :].astype(x.dtype), window_strides=(1,), padding=[(k - 1, 0)],
        dimension_numbers=('NWC', 'WIO', 'NWC'), feature_group_count=c)


def chunk_gated_delta_rule(q, k, v, g, beta):
    f32 = jnp.float32
    b, t, h, dk = q.shape
    dv = v.shape[-1]
    c = GDN_CHUNK
    n = t // c
    q = l2_normalize(q.astype(f32)) * dk ** -0.5
    k = l2_normalize(k.astype(f32))
    v = v.astype(f32)

    def chunks(z):
        return z.reshape(b, n, c, h, -1).transpose(1, 0, 3, 2, 4)

    q, k, v = chunks(q), chunks(k), chunks(v)
    g = jnp.cumsum(g.reshape(b, n, c, h).transpose(1, 0, 3, 2), axis=-1)
    beta = beta.reshape(b, n, c, h).transpose(1, 0, 3, 2)
    causal = np.tril(np.ones((c, c), dtype=bool))
    strict = np.tril(np.ones((c, c), dtype=bool), -1)
    gdiff = g[..., :, None] - g[..., None, :]
    decay = jnp.where(causal, jnp.exp(jnp.where(causal, gdiff, 0.0)), 0.0)
    k_beta = k * beta[..., None]
    v_beta = v * beta[..., None]
    low = jnp.where(strict, jnp.einsum('nbhcd,nbhsd->nbhcs', k_beta, k) * decay, 0.0)
    rhs = jnp.concatenate([v_beta, k_beta * jnp.exp(g)[..., None]], axis=-1)
    sol = lax.linalg.triangular_solve(low + jnp.eye(c, dtype=f32), rhs, left_side=True,
                                      lower=True, unit_diagonal=True)
    u, w = sol[..., :dv], sol[..., dv:]
    attn_intra = jnp.where(causal, jnp.einsum('nbhcd,nbhsd->nbhcs', q, k) * decay, 0.0)

    def step(state, xs):
        q_i, k_i, u_i, w_i, a_i, g_i = xs
        v_new = u_i - jnp.einsum('bhck,bhkv->bhcv', w_i, state)
        o_i = (jnp.einsum('bhck,bhkv->bhcv', q_i * jnp.exp(g_i)[..., None], state)
               + jnp.einsum('bhcs,bhsv->bhcv', a_i, v_new))
        g_last = g_i[..., -1]
        k_dec = k_i * jnp.exp(g_last[..., None] - g_i)[..., None]
        state = state * jnp.exp(g_last)[..., None, None] + jnp.einsum('bhck,bhcv->bhkv', k_dec, v_new)
        return state, o_i

    state0 = jnp.zeros((b, h, dk, dv), f32)
    _, o = lax.scan(step, state0, (q, k, u, w, attn_intra, g))
    return o.transpose(1, 0, 3, 2, 4).reshape(b, t, h, dv)


def gated_deltanet(h, w_in, conv_w, a_log, dt_bias, norm_w, w_out):
    f32 = jnp.float32
    b, t, _ = h.shape
    qkv_dim = 2 * GDN_KEY_DIM + GDN_VAL_DIM
    qkv, z, beta_in, decay_in = jnp.split(
        h @ w_in, [qkv_dim, qkv_dim + GDN_VAL_DIM, qkv_dim + GDN_VAL_DIM + GDN_HEADS], axis=-1)
    qkv = jax.nn.silu(causal_depthwise_conv(qkv, conv_w))
    q, k, v = jnp.split(qkv, [GDN_KEY_DIM, 2 * GDN_KEY_DIM], axis=-1)
    q = q.reshape(b, t, GDN_HEADS, GDN_HEAD_K)
    k = k.reshape(b, t, GDN_HEADS, GDN_HEAD_K)
    v = v.reshape(b, t, GDN_HEADS, GDN_HEAD_V)
    beta = jax.nn.sigmoid(beta_in.astype(f32))
    g = -jnp.exp(a_log.astype(f32)) * jax.nn.softplus(decay_in.astype(f32) + dt_bias.astype(f32))
    o = chunk_gated_delta_rule(q, k, v, g, beta)
    gate = jax.nn.silu(z.reshape(b, t, GDN_HEADS, GDN_HEAD_V).astype(f32))
    o = rms_norm(o, norm_w) * gate
    return o.reshape(b, t, GDN_VAL_DIM).astype(h.dtype) @ w_out


def stick_breaking_attention(h, w_in, w_out):
    f32 = jnp.float32
    b, t, _ = h.shape
    q, k, v = jnp.split(h @ w_in, 3, axis=-1)
    q = q.reshape(b, t, SB_HEADS, SB_HEAD_DIM)
    k = k.reshape(b, t, SB_HEADS, SB_HEAD_DIM)
    v = v.reshape(b, t, SB_HEADS, SB_HEAD_DIM)
    scale = SB_HEAD_DIM ** -0.5
    outs = []
    for blk in range(t // Q_BLOCK):
        t0 = blk * Q_BLOCK
        end = t0 + Q_BLOCK
        z = jnp.einsum('bqhd,bshd->bhqs', q[:, t0:end], k[:, :end]).astype(f32) * scale
        valid = np.arange(end)[None, :] < (t0 + np.arange(Q_BLOCK))[:, None]
        log_not = jnp.where(valid, -jax.nn.softplus(z), 0.0)
        between = lax.cumsum(log_not, axis=3, reverse=True) - log_not
        a = jnp.where(valid, jnp.exp(jax.nn.log_sigmoid(z) + between), 0.0)
        outs.append(jnp.einsum('bhqs,bshd->bqhd', a.astype(v.dtype), v[:, :end]))
    o = jnp.concatenate(outs, axis=1).reshape(b, t, SB_HEADS * SB_HEAD_DIM)
    return o @ w_out


def t5_bucket(dist):
    exact = REL_BUCKETS // 2
    n = np.maximum(dist, 0)
    log_ratio = (np.log(np.maximum(n, 1).astype(np.float32) / exact)
                 / np.log(np.float32(REL_MAX_DIST / exact)))
    large = np.minimum(exact + (log_ratio * (REL_BUCKETS - exact)).astype(np.int32), REL_BUCKETS - 1)
    return np.where(n < exact, n, large).astype(np.int32)


def sliding_window_attention(h, w_in, sinks, rel_bias, w_out):
    f32 = jnp.float32
    b, t, _ = h.shape
    nb = t // Q_BLOCK
    grp = SWA_Q_HEADS // SWA_KV_HEADS
    qd = SWA_Q_HEADS * SWA_HEAD_DIM
    kd = SWA_KV_HEADS * SWA_HEAD_DIM
    q, k, v = jnp.split(h @ w_in, [qd, qd + kd], axis=-1)
    q = q.reshape(b, nb, Q_BLOCK, SWA_KV_HEADS, grp, SWA_HEAD_DIM)

    def band(z):
        z = z.reshape(b, t, SWA_KV_HEADS, SWA_HEAD_DIM)
        z = jnp.pad(z, ((0, 0), (Q_BLOCK, 0), (0, 0), (0, 0)))
        z = z.reshape(b, nb + 1, Q_BLOCK, SWA_KV_HEADS, SWA_HEAD_DIM)
        return jnp.concatenate([z[:, :-1], z[:, 1:]], axis=2)

    kb, vb = band(k), band(v)
    scores = jnp.einsum('bnqkgd,bnskd->bnkgqs', q, kb).astype(f32) * SWA_HEAD_DIM ** -0.5
    dist = np.arange(Q_BLOCK)[:, None] + Q_BLOCK - np.arange(2 * Q_BLOCK)[None, :]
    bias = rel_bias.astype(f32)[t5_bucket(dist)]
    bias = bias.transpose(2, 0, 1).reshape(SWA_KV_HEADS, grp, Q_BLOCK, 2 * Q_BLOCK)
    key_pos = np.arange(nb)[:, None] * Q_BLOCK + np.arange(2 * Q_BLOCK)[None, :] - Q_BLOCK
    mask = ((dist >= 0) & (dist < SWA_WINDOW))[None] & (key_pos >= 0)[:, None, :]
    scores = jnp.where(mask[None, :, None, None], scores + bias, NEG_INF)
    sink = sinks.astype(f32).reshape(SWA_KV_HEADS, grp)[None, None, :, :, None, None]
    m = jnp.maximum(jnp.max(scores, axis=-1, keepdims=True), sink)
    e = jnp.exp(scores - m)
    p = e / (jnp.sum(e, axis=-1, keepdims=True) + jnp.exp(sink - m))
    o = jnp.einsum('bnkgqs,bnskd->bnqkgd', p.astype(vb.dtype), vb).reshape(b, t, qd)
    return o @ w_out


def apply_rope(x, cos, sin):
    half = x.shape[-1] // 2
    x32 = x.astype(jnp.float32)
    x1, x2 = x32[..., :half], x32[..., half:]
    return jnp.concatenate([x1 * cos - x2 * sin, x2 * cos + x1 * sin], axis=-1).astype(x.dtype)


def multi_head_latent_attention(h, positions, w_in, q_norm, w_q_b, kv_norm, w_kv_b, w_out):
    f32 = jnp.float32
    b, t, _ = h.shape
    cq, ckv, k_rope = jnp.split(h @ w_in, [MLA_Q_RANK, MLA_Q_RANK + MLA_KV_RANK], axis=-1)
    q = (rms_norm(cq, q_norm) @ w_q_b).reshape(b, t, MLA_HEADS, MLA_NOPE + MLA_ROPE)
    kv = (rms_norm(ckv, kv_norm) @ w_kv_b).reshape(b, t, MLA_HEADS, MLA_NOPE + MLA_V)
    q_nope, q_rope = q[..., :MLA_NOPE], q[..., MLA_NOPE:]
    k_nope, v = kv[..., :MLA_NOPE], kv[..., MLA_NOPE:]
    half = MLA_ROPE // 2
    inv_freq = ROPE_THETA ** (-jnp.arange(half, dtype=f32) / half)
    ang = positions.astype(f32)[..., None] * inv_freq
    cos, sin = jnp.cos(ang), jnp.sin(ang)
    q_rope = apply_rope(q_rope, cos[:, :, None], sin[:, :, None])
    k_rope = apply_rope(k_rope, cos, sin)
    scale = (MLA_NOPE + MLA_ROPE) ** -0.5
    outs = []
    for blk in range(t // Q_BLOCK):
        t0 = blk * Q_BLOCK
        end = t0 + Q_BLOCK
        s = (jnp.einsum('bqhd,bshd->bhqs', q_nope[:, t0:end], k_nope[:, :end])
             + jnp.einsum('bqhr,bsr->bhqs', q_rope[:, t0:end], k_rope[:, :end])).astype(f32) * scale
        causal = np.arange(end)[None, :] <= (t0 + np.arange(Q_BLOCK))[:, None]
        p = jax.nn.softmax(jnp.where(causal, s, NEG_INF), axis=-1)
        outs.append(jnp.einsum('bhqs,bshd->bqhd', p.astype(v.dtype), v[:, :end]))
    o = jnp.concatenate(outs, axis=1).reshape(b, t, MLA_HEADS * MLA_V)
    return o @ w_out


def swiglu(h, w_gate_up, w_down):
    gate, up = jnp.split(h @ w_gate_up, 2, axis=-1)
    return (jax.nn.silu(gate) * up) @ w_down


def routed_swiglu(h, w_router, w_gate_up, w_down):
    logits = (h @ w_router).astype(jnp.float32)
    top_v, top_i = lax.top_k(logits, TOP_K)
    top_w = jax.nn.softmax(top_v, axis=-1)
    gates = jnp.sum(jax.nn.one_hot(top_i, N_EXPERTS, dtype=jnp.float32) * top_w[..., None], axis=-2)
    gates = gates.astype(h.dtype)
    out = jnp.zeros_like(h)
    for e in range(N_EXPERTS):
        out = out + gates[..., e:e + 1] * swiglu(h, w_gate_up[e], w_down[e])
    return out


def setup_inputs(seed: int = 0) -> dict:
    key = jax.random.key(seed)
    ks = jax.random.split(key, 32)
    f32 = jnp.float32
    d = D_MODEL

    def dense(i, shape, fan_in):
        return jax.random.normal(ks[i], shape, f32) * fan_in ** -0.5

    def gain(i, shape):
        return 1.0 + 0.1 * jax.random.normal(ks[i], shape, f32)

    x = jax.random.normal(ks[0], (BATCH, SEQ, d), f32)
    positions = (jnp.arange(SEQ, dtype=jnp.int32)[None, :]
                 + jax.random.randint(ks[1], (BATCH, 1), 0, 1024, dtype=jnp.int32))
    rel_bias = 0.5 * jax.random.normal(ks[2], (REL_BUCKETS, SWA_Q_HEADS), f32)
    attn_norm = gain(3, (DEPTH, d))
    ffn_norm = gain(4, (DEPTH, d))
    final_norm = gain(5, (d,))
    gdn_w_in = dense(6, (N_GDN, d, 2 * GDN_KEY_DIM + 2 * GDN_VAL_DIM + 2 * GDN_HEADS), d)
    gdn_conv_w = dense(7, (N_GDN, GDN_CONV, 2 * GDN_KEY_DIM + GDN_VAL_DIM), GDN_CONV)
    gdn_a_log = jnp.log(jax.random.uniform(ks[8], (N_GDN, GDN_HEADS), f32, 1.0, 16.0))
    dt = jnp.exp(jax.random.uniform(ks[9], (N_GDN, GDN_HEADS), f32, math.log(1e-3), math.log(1e-1)))
    gdn_dt_bias = dt + jnp.log(-jnp.expm1(-dt))
    gdn_norm_w = gain(10, (N_GDN, GDN_HEAD_V))
    gdn_w_out = dense(11, (N_GDN, GDN_VAL_DIM, d), GDN_VAL_DIM)
    sb_w_in = dense(12, (N_SB, d, 3 * SB_HEADS * SB_HEAD_DIM), d)
    sb_w_out = dense(13, (N_SB, SB_HEADS * SB_HEAD_DIM, d), SB_HEADS * SB_HEAD_DIM)
    swa_w_in = dense(14, (N_SWA, d, (SWA_Q_HEADS + 2 * SWA_KV_HEADS) * SWA_HEAD_DIM), d)
    swa_sinks = jax.random.normal(ks[15], (N_SWA, SWA_Q_HEADS), f32)
    swa_w_out = dense(16, (N_SWA, SWA_Q_HEADS * SWA_HEAD_DIM, d), SWA_Q_HEADS * SWA_HEAD_DIM)
    mla_w_in = dense(17, (N_MLA, d, MLA_Q_RANK + MLA_KV_RANK + MLA_ROPE), d)
    mla_q_norm = gain(18, (N_MLA, MLA_Q_RANK))
    mla_w_q_b = dense(19, (N_MLA, MLA_Q_RANK, MLA_HEADS * (MLA_NOPE + MLA_ROPE)), MLA_Q_RANK)
    mla_kv_norm = gain(20, (N_MLA, MLA_KV_RANK))
    mla_w_kv_b = dense(21, (N_MLA, MLA_KV_RANK, MLA_HEADS * (MLA_NOPE + MLA_V)), MLA_KV_RANK)
    mla_w_out = dense(22, (N_MLA, MLA_HEADS * MLA_V, d), MLA_HEADS * MLA_V)
    ffn_w_gate_up = dense(23, (N_DENSE, d, 2 * FFN_DENSE), d)
    ffn_w_down = dense(24, (N_DENSE, FFN_DENSE, d), FFN_DENSE)
    moe_w_router = dense(25, (N_MOE, d, N_EXPERTS), d)
    moe_w_gate_up = dense(26, (N_MOE, N_EXPERTS, d, 2 * FFN_EXPERT), d)
    moe_w_down = dense(27, (N_MOE, N_EXPERTS, FFN_EXPERT, d), FFN_EXPERT)
    return {
        "x": x, "positions": positions, "rel_bias": rel_bias,
        "attn_norm": attn_norm, "ffn_norm": ffn_norm, "final_norm": final_norm,
        "gdn_w_in": gdn_w_in, "gdn_conv_w": gdn_conv_w, "gdn_a_log": gdn_a_log,
        "gdn_dt_bias": gdn_dt_bias, "gdn_norm_w": gdn_norm_w, "gdn_w_out": gdn_w_out,
        "sb_w_in": sb_w_in, "sb_w_out": sb_w_out,
        "swa_w_in": swa_w_in, "swa_sinks": swa_sinks, "swa_w_out": swa_w_out,
        "mla_w_in": mla_w_in, "mla_q_norm": mla_q_norm, "mla_w_q_b": mla_w_q_b,
        "mla_kv_norm": mla_kv_norm, "mla_w_kv_b": mla_w_kv_b, "mla_w_out": mla_w_out,
        "ffn_w_gate_up": ffn_w_gate_up, "ffn_w_down": ffn_w_down,
        "moe_w_router": moe_w_router, "moe_w_gate_up": moe_w_gate_up, "moe_w_down": moe_w_down,
    }


def reference(x, positions, rel_bias, attn_norm, ffn_norm, final_norm,
              gdn_w_in, gdn_conv_w, gdn_a_log, gdn_dt_bias, gdn_norm_w, gdn_w_out,
              sb_w_in, sb_w_out, swa_w_in, swa_sinks, swa_w_out,
              mla_w_in, mla_q_norm, mla_w_q_b, mla_kv_norm, mla_w_kv_b, mla_w_out,
              ffn_w_gate_up, ffn_w_down, moe_w_router, moe_w_gate_up, moe_w_down):
    h = x
    for i in range(DEPTH):
        mixer, occ = i % N_MIXERS, i // N_MIXERS
        y = rms_norm(h, attn_norm[i])
        if mixer == 0:
            y = gated_deltanet(y, gdn_w_in[occ], gdn_conv_w[occ], gdn_a_log[occ],
                               gdn_dt_bias[occ], gdn_norm_w[occ], gdn_w_out[occ])
        elif mixer == 1:
            y = stick_breaking_attention(y, sb_w_in[occ], sb_w_out[occ])
        elif mixer == 2:
            y = sliding_window_attention(y, swa_w_in[occ], swa_sinks[occ], rel_bias, swa_w_out[occ])
        else:
            y = multi_head_latent_attention(y, positions, mla_w_in[occ], mla_q_norm[occ], mla_w_q_b[occ],
                                            mla_kv_norm[occ], mla_w_kv_b[occ], mla_w_out[occ])
        h = h + y
        y = rms_norm(h, ffn_norm[i])
        f = i // 2
        if i % 2 == 0:
            y = swiglu(y, ffn_w_gate_up[f], ffn_w_down[f])
        else:
            y = routed_swiglu(y, moe_w_router[f], moe_w_gate_up[f], moe_w_down[f])
        h = h + y
    return rms_norm(h, final_norm)
```

```python
import functools
import math

import numpy as np
import jax
import jax.numpy as jnp
from jax import lax
from jax.experimental import pallas as pl
from jax.experimental.pallas import tpu as pltpu

F32 = jnp.float32
BF16 = jnp.bfloat16
HIGHEST = lax.Precision.HIGHEST

NORM_EPS = 1e-6
NEG_INF = -1e30
LANES = 128

GDN_HEADS, GDN_HEAD_DIM, GDN_CONV = 8, 128, 4
GDN_BLOCK = 256
GDN_INV_BASE = 16
SB_HEADS, SB_HEAD_DIM, SB_BLOCK = 16, 64, 256
SWA_Q_HEADS, SWA_KV_HEADS, SWA_HEAD_DIM, SWA_WINDOW, SWA_QBLOCK = 16, 2, 64, 128, 128
REL_BUCKETS, REL_MAX_DIST = 32, 128
MLA_HEADS, MLA_Q_RANK, MLA_KV_RANK, MLA_NOPE, MLA_ROPE, MLA_V = 8, 384, 256, 128, 64, 128
ROPE_THETA = 10000.0
N_EXPERTS, TOP_K = 8, 2
MOE_TILE = 1024


def _cparams(sem, vmem_mb):
    return pltpu.CompilerParams(dimension_semantics=sem, vmem_limit_bytes=vmem_mb << 20)


def _dot(a, b):
    return jnp.dot(a, b, preferred_element_type=F32)


def _dot_nt(a, b):
    return lax.dot_general(a, b, (((1,), (1,)), ((), ())), preferred_element_type=F32)


def _dot_hi(a, b):
    return jnp.dot(a, b, preferred_element_type=F32, precision=HIGHEST)


def _dot_nt_hi(a, b):
    return lax.dot_general(a, b, (((1,), (1,)), ((), ())), preferred_element_type=F32, precision=HIGHEST)


def _rms(x, w):
    return x * lax.rsqrt(jnp.mean(x * x, axis=-1, keepdims=True) + NORM_EPS) * w


def _silu(x):
    return x * jax.nn.sigmoid(x)


def _softplus(x):
    return jnp.maximum(x, 0.0) + jnp.log(1.0 + jnp.exp(-jnp.abs(x)))


def _norm_matmul_kernel(x_ref, nw_ref, w_ref, o_ref, xn_ref):
    @pl.when(pl.program_id(1) == 0)
    def _():
        xn_ref[...] = _rms(x_ref[...], nw_ref[...]).astype(BF16)

    o_ref[...] = _dot(xn_ref[...], w_ref[...]).astype(o_ref.dtype)


def norm_matmul(x, nw, w, *, tm=1024, tn=512, out_dtype=BF16):
    m, d = x.shape
    n = w.shape[1]
    tm, tn = min(tm, m), min(tn, n)
    return pl.pallas_call(
        _norm_matmul_kernel,
        grid=(m // tm, n // tn),
        in_specs=[pl.BlockSpec((tm, d), lambda i, j: (i, 0)),
                  pl.BlockSpec((1, d), lambda i, j: (0, 0)),
                  pl.BlockSpec((d, tn), lambda i, j: (0, j))],
        out_specs=pl.BlockSpec((tm, tn), lambda i, j: (i, j)),
        out_shape=jax.ShapeDtypeStruct((m, n), out_dtype),
        scratch_shapes=[pltpu.VMEM((tm, d), BF16)],
        compiler_params=_cparams(("parallel", "arbitrary"), 40),
        name="norm_matmul",
    )(x, nw.reshape(1, d), w)


def _matmul_residual_kernel(a_ref, w_ref, h_ref, o_ref):
    o_ref[...] = h_ref[...] + _dot(a_ref[...], w_ref[...])


def matmul_residual(a, w, h, *, tm=512):
    m, k = a.shape
    d = w.shape[1]
    tm = min(tm, m)
    return pl.pallas_call(
        _matmul_residual_kernel,
        grid=(m // tm,),
        in_specs=[pl.BlockSpec((tm, k), lambda i: (i, 0)),
                  pl.BlockSpec((k, d), lambda i: (0, 0)),
                  pl.BlockSpec((tm, d), lambda i: (i, 0))],
        out_specs=pl.BlockSpec((tm, d), lambda i: (i, 0)),
        out_shape=jax.ShapeDtypeStruct((m, d), F32),
        compiler_params=_cparams(("parallel",), 40),
        name="matmul_residual",
    )(a, w, h)


def _ffn_kernel(x_ref, nw_ref, wg_ref, wu_ref, wd_ref, o_ref, xn_ref, acc_ref):
    j = pl.program_id(1)

    @pl.when(j == 0)
    def _():
        x = x_ref[...]
        xn_ref[...] = _rms(x, nw_ref[...]).astype(BF16)
        acc_ref[...] = x

    xn = xn_ref[...]
    a = (_silu(_dot(xn, wg_ref[...])) * _dot(xn, wu_ref[...])).astype(BF16)
    acc_ref[...] += _dot(a, wd_ref[...])

    @pl.when(j == pl.num_programs(1) - 1)
    def _():
        o_ref[...] = acc_ref[...]


def ffn_dense(h, nw, w_gate_up, w_down, *, tm=1024, tf=256):
    m, d = h.shape
    f = w_down.shape[0]
    tm, tf = min(tm, m), min(tf, f)
    nf = f // tf
    return pl.pallas_call(
        _ffn_kernel,
        grid=(m // tm, nf),
        in_specs=[pl.BlockSpec((tm, d), lambda i, j: (i, 0)),
                  pl.BlockSpec((1, d), lambda i, j: (0, 0)),
                  pl.BlockSpec((d, tf), lambda i, j: (0, j)),
                  pl.BlockSpec((d, tf), lambda i, j: (0, j + nf)),
                  pl.BlockSpec((tf, d), lambda i, j: (j, 0))],
        out_specs=pl.BlockSpec((tm, d), lambda i, j: (i, 0)),
        out_shape=jax.ShapeDtypeStruct((m, d), F32),
        scratch_shapes=[pltpu.VMEM((tm, d), BF16), pltpu.VMEM((tm, d), F32)],
        compiler_params=_cparams(("parallel", "arbitrary"), 48),
        name="ffn_dense",
    )(h, nw.reshape(1, d), w_gate_up, w_gate_up, w_down)


def _gdn_gates_kernel(x_ref, nw_ref, w_ref, wt_ref, pcol_ref, prow_ref, col_ref, row_ref):
    tm = x_ref.shape[0]
    nh = GDN_HEADS
    xn = _rms(x_ref[...], nw_ref[...])
    lc = _dot_hi(xn, w_ref[...])
    lr = _dot_nt_hi(wt_ref[...], xn)[:2 * nh]
    ci = lax.broadcasted_iota(jnp.int32, lc.shape, 1)
    is_g_c = (ci >= nh) & (ci < 2 * nh)
    g_c = jnp.where(is_g_c, pcol_ref[0:1, :] * _softplus(lc + pcol_ref[1:2, :]), 0.0)
    ri = lax.broadcasted_iota(jnp.int32, lr.shape, 0)
    g_r = jnp.where(ri >= nh, prow_ref[:, 0:1] * _softplus(lr + prow_ref[:, 1:2]), 0.0)
    r = lax.broadcasted_iota(jnp.int32, (tm, tm), 0)
    c = lax.broadcasted_iota(jnp.int32, (tm, tm), 1)
    same = (r // GDN_BLOCK) == (c // GDN_BLOCK)
    lower = jnp.where(same & (c <= r), 1.0, 0.0).astype(F32)
    upper = jnp.where(same & (r <= c), 1.0, 0.0).astype(F32)
    col_ref[...] = jnp.where(ci < nh, jax.nn.sigmoid(lc), _dot_hi(lower, g_c))
    row_ref[...] = jnp.where(ri < nh, jax.nn.sigmoid(lr), _dot_hi(g_r, upper))


def gdn_gates(h, nw, w_bd, a_log, dt_bias, *, tm=512):
    m, d = h.shape
    tm = min(tm, m)
    nh = GDN_HEADS
    w_pad = jnp.zeros((d, LANES), F32).at[:, :2 * nh].set(w_bd)
    z = jnp.zeros((nh,), F32)
    prm = jnp.stack([jnp.concatenate([z, -jnp.exp(a_log.astype(F32))]),
                     jnp.concatenate([z, dt_bias.astype(F32)])])
    pcol = jnp.zeros((8, LANES), F32).at[:2, :2 * nh].set(prm)
    prow = jnp.zeros((2 * nh, LANES), F32).at[:, :2].set(prm.T)
    return pl.pallas_call(
        _gdn_gates_kernel,
        grid=(m // tm,),
        in_specs=[pl.BlockSpec((tm, d), lambda i: (i, 0)),
                  pl.BlockSpec((1, d), lambda i: (0, 0)),
                  pl.BlockSpec((d, LANES), lambda i: (0, 0)),
                  pl.BlockSpec((LANES, d), lambda i: (0, 0)),
                  pl.BlockSpec((8, LANES), lambda i: (0, 0)),
                  pl.BlockSpec((2 * nh, LANES), lambda i: (0, 0))],
        out_specs=[pl.BlockSpec((tm, LANES), lambda i: (i, 0)),
                   pl.BlockSpec((2 * nh, tm), lambda i: (0, i))],
        out_shape=[jax.ShapeDtypeStruct((m, LANES), F32), jax.ShapeDtypeStruct((2 * nh, m), F32)],
        compiler_params=_cparams(("parallel",), 40),
        name="gdn_gates",
    )(h, nw.reshape(1, d), w_pad, w_pad.T, pcol, prow)


def _unit_lower_inverse(low):
    n = low.shape[0]
    r = lax.broadcasted_iota(jnp.int32, (n, n), 0)
    c = lax.broadcasted_iota(jnp.int32, (n, n), 1)
    eye = jnp.where(r == c, 1.0, 0.0).astype(F32)
    b = GDN_INV_BASE
    nb = jnp.where((r // b) == (c // b), -low, 0.0)
    n2 = _dot_hi(nb, nb)
    n4 = _dot_hi(n2, n2)
    n8 = _dot_hi(n4, n4)
    p = eye + nb + n2 + _dot_hi(nb, n2)
    q = eye + n4 + n8 + _dot_hi(n4, n8)
    inv = _dot_hi(p, q)
    while b < n:
        m = jnp.where(((r // (2 * b)) == (c // (2 * b))) & ((r // b) != (c // b)), low, 0.0)
        inv = inv - _dot_hi(inv, _dot_hi(m, inv))
        b *= 2
    return inv


def _gdn_core_kernel(q_ref, k_ref, v_ref, z_ref, cwq_ref, cwk_ref, cwv_ref, col_ref, row_ref, nw_ref,
                     o_ref, s_ref, tail_ref, ext_ref):
    h = pl.program_id(1)
    t = pl.program_id(2)
    tb, dk = q_ref.shape
    nh = GDN_HEADS

    @pl.when(t == 0)
    def _():
        s_ref[...] = jnp.zeros_like(s_ref)
        tail_ref[...] = jnp.zeros_like(tail_ref)

    def conv_silu(i, x_ref, cw_ref):
        x = x_ref[...].astype(F32)
        ext_ref[0:8, :] = tail_ref[i]
        ext_ref[8:, :] = x
        tail_ref[i] = x[tb - 8:, :]
        acc = x * cw_ref[GDN_CONV - 1:GDN_CONV, :]
        for s in range(1, GDN_CONV):
            acc = acc + ext_ref[8 - s:8 - s + tb, :] * cw_ref[GDN_CONV - 1 - s:GDN_CONV - s, :]
        return _silu(acc)

    q = conv_silu(0, q_ref, cwq_ref)
    k = conv_silu(1, k_ref, cwk_ref)
    v = conv_silu(2, v_ref, cwv_ref)
    qn = q * (lax.rsqrt(jnp.sum(q * q, axis=-1, keepdims=True) + NORM_EPS) * dk ** -0.5)
    kn = k * lax.rsqrt(jnp.sum(k * k, axis=-1, keepdims=True) + NORM_EPS)

    colv = col_ref[...]
    ci = lax.broadcasted_iota(jnp.int32, colv.shape, 1)
    beta = jnp.sum(jnp.where(ci == h, colv, 0.0), axis=1, keepdims=True)
    gc = jnp.sum(jnp.where(ci == h + nh, colv, 0.0), axis=1, keepdims=True)
    gr = row_ref[pl.ds(h + nh, 1), :]
    g_last = gr[:, tb - 1:tb]

    r = lax.broadcasted_iota(jnp.int32, (tb, tb), 0)
    c = lax.broadcasted_iota(jnp.int32, (tb, tb), 1)
    incl = c <= r
    decay = jnp.where(incl, jnp.exp(jnp.where(incl, gc - gr, 0.0)), 0.0)
    kb = kn * beta
    low = jnp.where(c < r, _dot_nt_hi(kb, kn) * decay, 0.0)
    tinv = _unit_lower_inverse(low)
    rhs = jnp.concatenate([v * beta, kb * jnp.exp(gc)], axis=1)
    uw = _dot_hi(tinv, rhs)
    u, w = uw[:, :dk], uw[:, dk:]
    attn = (_dot_nt(qn.astype(BF16), kn.astype(BF16)) * decay).astype(BF16)

    s = s_ref[...]
    sb = s.astype(BF16)
    v_new = u - _dot(w.astype(BF16), sb)
    o = _dot((qn * jnp.exp(gc)).astype(BF16), sb) + _dot(attn, v_new.astype(BF16))
    k_dec = kn * jnp.exp(g_last - gc)
    s_ref[...] = s * jnp.exp(g_last) + _dot(k_dec.T.astype(BF16), v_new.astype(BF16))

    o = _rms(o, nw_ref[...]) * _silu(z_ref[...].astype(F32))
    o_ref[...] = o.astype(o_ref.dtype)


def gdn_core(qkvz, conv_w, gcol, grow, norm_w, batch, seq):
    n = qkvz.shape[0]
    nh, dk, tb = GDN_HEADS, GDN_HEAD_DIM, min(GDN_BLOCK, seq)
    nt = seq // tb

    def tok(off):
        return pl.BlockSpec((tb, dk), lambda b, h, t: (b * nt + t, h + off))

    def cw(off):
        return pl.BlockSpec((GDN_CONV, dk), lambda b, h, t: (0, h + off))

    return pl.pallas_call(
        _gdn_core_kernel,
        grid=(batch, nh, nt),
        in_specs=[tok(0), tok(nh), tok(2 * nh), tok(3 * nh), cw(0), cw(nh), cw(2 * nh),
                  pl.BlockSpec((tb, LANES), lambda b, h, t: (b * nt + t, 0)),
                  pl.BlockSpec((2 * nh, tb), lambda b, h, t: (0, b * nt + t)),
                  pl.BlockSpec((1, dk), lambda b, h, t: (0, 0))],
        out_specs=pl.BlockSpec((tb, dk), lambda b, h, t: (b * nt + t, h)),
        out_shape=jax.ShapeDtypeStruct((n, nh * dk), BF16),
        scratch_shapes=[pltpu.VMEM((dk, dk), F32), pltpu.VMEM((3, 8, dk), F32), pltpu.VMEM((tb + 8, dk), F32)],
        compiler_params=_cparams(("parallel", "parallel", "arbitrary"), 40),
        name="gdn_core",
    )(qkvz, qkvz, qkvz, qkvz, conv_w, conv_w, conv_w, gcol, grow, norm_w.reshape(1, dk))


def _sb_attn_kernel(q_ref, k_ref, v_ref, o_ref):
    i = pl.program_id(2)
    tq = q_ref.shape[0]
    hd = SB_HEAD_DIM
    scale = hd ** -0.5
    r = lax.broadcasted_iota(jnp.int32, (tq, tq), 0)
    c = lax.broadcasted_iota(jnp.int32, (tq, tq), 1)
    upper = jnp.where(r >= c, 1.0, 0.0).astype(BF16)
    strict = c < r

    outs = []
    for hh in range(LANES // hd):
        lo, hi = hh * hd, (hh + 1) * hd
        qh = q_ref[:, lo:hi]

        def block(kb, carry, acc, masked):
            start = pl.multiple_of(kb * tq, tq)
            kblk = k_ref[pl.ds(start, tq), lo:hi]
            vblk = v_ref[pl.ds(start, tq), lo:hi]
            z = _dot_nt(qh, kblk) * scale
            ln = -_softplus(z)
            if masked:
                ln = jnp.where(strict, ln, 0.0)
            ln_hi = ln.astype(BF16)
            ln_lo = (ln - ln_hi.astype(F32)).astype(BF16)
            rr = _dot(jnp.concatenate([ln_hi, ln_lo], axis=0), upper)
            rloc = rr[:tq] + rr[tq:]
            a = jnp.exp(z + rloc + carry)
            if masked:
                a = jnp.where(strict, a, 0.0)
            acc = acc + _dot(a.astype(BF16), vblk)
            return carry + rloc[:, 0:1], acc

        carry, acc = block(i, jnp.zeros((tq, 1), F32), jnp.zeros((tq, hd), F32), True)

        def body(n, st):
            return block(i - 1 - n, st[0], st[1], False)

        carry, acc = lax.fori_loop(0, i, body, (carry, acc))
        outs.append(acc)
    o_ref[...] = jnp.concatenate(outs, axis=1).astype(o_ref.dtype)


def sb_attention(qkv, batch, seq):
    n = qkv.shape[0]
    tq = min(SB_BLOCK, seq)
    nq = seq // tq
    npair = SB_HEADS * SB_HEAD_DIM // LANES
    return pl.pallas_call(
        _sb_attn_kernel,
        grid=(batch, npair, nq),
        in_specs=[pl.BlockSpec((tq, LANES), lambda b, p, i: (b * nq + i, p)),
                  pl.BlockSpec((seq, LANES), lambda b, p, i: (b, npair + p)),
                  pl.BlockSpec((seq, LANES), lambda b, p, i: (b, 2 * npair + p))],
        out_specs=pl.BlockSpec((tq, LANES), lambda b, p, i: (b * nq + i, p)),
        out_shape=jax.ShapeDtypeStruct((n, SB_HEADS * SB_HEAD_DIM), BF16),
        compiler_params=_cparams(("parallel", "parallel", "arbitrary"), 40),
        name="sb_attention",
    )(qkv, qkv, qkv)


def _t5_bucket(dist):
    exact = REL_BUCKETS // 2
    n = np.maximum(dist, 0)
    log_ratio = (np.log(np.maximum(n, 1).astype(np.float32) / exact)
                 / np.log(np.float32(REL_MAX_DIST / exact)))
    large = np.minimum(exact + (log_ratio * (REL_BUCKETS - exact)).astype(np.int32), REL_BUCKETS - 1)
    return np.where(n < exact, n, large).astype(np.int32)


def _swa_kernel(sink_ref, q_ref, kc_ref, kp_ref, vc_ref, vp_ref, bias_ref, o_ref):
    nblk = pl.program_id(1)
    tq = q_ref.shape[0]
    hd = SWA_HEAD_DIM
    grp = SWA_Q_HEADS // SWA_KV_HEADS
    scale = hd ** -0.5
    r = lax.broadcasted_iota(jnp.int32, (tq, 2 * tq), 0)
    c = lax.broadcasted_iota(jnp.int32, (tq, 2 * tq), 1)
    dist = r + tq - c
    mask = (dist >= 0) & (dist < SWA_WINDOW) & ((c >= tq) | (nblk > 0))
    outs = []
    for kv in range(SWA_KV_HEADS):
        lo, hi = kv * hd, (kv + 1) * hd
        kb = jnp.concatenate([kp_ref[:, lo:hi], kc_ref[:, lo:hi]], axis=0)
        vb = jnp.concatenate([vp_ref[:, lo:hi], vc_ref[:, lo:hi]], axis=0)
        for g in range(grp):
            hq = kv * grp + g
            qh = q_ref[:, hq * hd:(hq + 1) * hd]
            s = _dot_nt(qh, kb) * scale + bias_ref[hq]
            s = jnp.where(mask, s, NEG_INF)
            sink = sink_ref[hq]
            m = jnp.maximum(jnp.max(s, axis=-1, keepdims=True), sink)
            e = jnp.exp(s - m)
            p = e / (jnp.sum(e, axis=-1, keepdims=True) + jnp.exp(sink - m))
            outs.append(_dot(p.astype(BF16), vb))
    o_ref[...] = jnp.concatenate(outs, axis=1).astype(o_ref.dtype)


def swa_attention(qkv, sinks, rel_bias, batch, seq):
    n = qkv.shape[0]
    tq = SWA_QBLOCK
    nb = seq // tq
    qd = SWA_Q_HEADS * SWA_HEAD_DIM
    kcol = qd // LANES
    dist = np.arange(tq)[:, None] + tq - np.arange(2 * tq)[None, :]
    bias = rel_bias.astype(F32)[_t5_bucket(dist)].transpose(2, 0, 1)

    def cur(col):
        return pl.BlockSpec((tq, LANES), lambda b, i: (b * nb + i, col))

    def prev(col):
        return pl.BlockSpec((tq, LANES), lambda b, i: (b * nb + jnp.maximum(i - 1, 0), col))

    return pl.pallas_call(
        _swa_kernel,
        grid=(batch, nb),
        in_specs=[pl.BlockSpec(memory_space=pltpu.SMEM),
                  pl.BlockSpec((tq, qd), lambda b, i: (b * nb + i, 0)),
                  cur(kcol), prev(kcol), cur(kcol + 1), prev(kcol + 1),
                  pl.BlockSpec((SWA_Q_HEADS, tq, 2 * tq), lambda b, i: (0, 0, 0))],
        out_specs=pl.BlockSpec((tq, qd), lambda b, i: (b * nb + i, 0)),
        out_shape=jax.ShapeDtypeStruct((n, qd), BF16),
        compiler_params=_cparams(("parallel", "arbitrary"), 40),
        name="swa_attention",
    )(sinks.astype(F32), qkv, qkv, qkv, qkv, qkv, bias)


def _mla_qkv_kernel(c_ref, pos_ref, invf_ref, qn_ref, kvn_ref, wq_ref, wkv_ref, q_ref, k_ref, v_ref):
    nh, dn = MLA_HEADS, MLA_NOPE
    cq = c_ref[:, :MLA_Q_RANK].astype(F32)
    ckv = c_ref[:, MLA_Q_RANK:MLA_Q_RANK + MLA_KV_RANK].astype(F32)
    base = MLA_Q_RANK + MLA_KV_RANK
    kr_a = c_ref[:, base:base + LANES].astype(F32)
    kr_b = c_ref[:, base + LANES:base + 2 * LANES].astype(F32)
    ang = pos_ref[...].astype(F32) * invf_ref[...]
    cos, sin = jnp.cos(ang), jnp.sin(ang)
    scale = (MLA_NOPE + MLA_ROPE) ** -0.5
    q = _dot(_rms(cq, qn_ref[...]).astype(BF16), wq_ref[...]) * scale
    kv = _dot(_rms(ckv, kvn_ref[...]).astype(BF16), wkv_ref[...])
    k_rot = (kr_a * cos + kr_b * sin).astype(BF16)
    for h in range(nh):
        q_nope = q[:, h * dn:(h + 1) * dn]
        q_a = q[:, (nh + h) * dn:(nh + h + 1) * dn]
        q_b = q[:, (2 * nh + h) * dn:(2 * nh + h + 1) * dn]
        q_ref[:, 2 * h * dn:(2 * h + 1) * dn] = q_nope.astype(BF16)
        q_ref[:, (2 * h + 1) * dn:(2 * h + 2) * dn] = (q_a * cos + q_b * sin).astype(BF16)
        k_ref[:, 2 * h * dn:(2 * h + 1) * dn] = kv[:, h * dn:(h + 1) * dn].astype(BF16)
        k_ref[:, (2 * h + 1) * dn:(2 * h + 2) * dn] = k_rot
    v_ref[...] = kv[:, nh * dn:].astype(BF16)


def _rotate_half_cols(w):
    half = w.shape[-1] // 2
    return jnp.concatenate([-w[..., half:], w[..., :half]], axis=-1)


def mla_qkv(c, positions, q_norm, w_q_b, kv_norm, w_kv_b, *, tm=512):
    n = c.shape[0]
    tm = min(tm, n)
    nh, dn, dr, dv = MLA_HEADS, MLA_NOPE, MLA_ROPE, MLA_V
    wq = w_q_b.reshape(MLA_Q_RANK, nh, dn + dr)
    pad = jnp.zeros((MLA_Q_RANK, nh, LANES - dr), w_q_b.dtype)
    wq_rope = wq[:, :, dn:]
    wq_all = jnp.concatenate([
        wq[:, :, :dn].reshape(MLA_Q_RANK, nh * dn),
        jnp.concatenate([wq_rope, pad], axis=-1).reshape(MLA_Q_RANK, nh * LANES),
        jnp.concatenate([_rotate_half_cols(wq_rope), pad], axis=-1).reshape(MLA_Q_RANK, nh * LANES),
    ], axis=1).astype(BF16)
    wkv = w_kv_b.reshape(MLA_KV_RANK, nh, dn + dv)
    wkv_all = jnp.concatenate([wkv[:, :, :dn].reshape(MLA_KV_RANK, nh * dn),
                               wkv[:, :, dn:].reshape(MLA_KV_RANK, nh * dv)], axis=1).astype(BF16)
    half = dr // 2
    inv_freq = ROPE_THETA ** (-jnp.arange(half, dtype=F32) / half)
    invf = jnp.concatenate([inv_freq, inv_freq, jnp.zeros((LANES - dr,), F32)]).reshape(1, LANES)
    cw = c.shape[1]
    return pl.pallas_call(
        _mla_qkv_kernel,
        grid=(n // tm,),
        in_specs=[pl.BlockSpec((tm, cw), lambda i: (i, 0)),
                  pl.BlockSpec((tm, 1), lambda i: (i, 0)),
                  pl.BlockSpec((1, LANES), lambda i: (0, 0)),
                  pl.BlockSpec((1, MLA_Q_RANK), lambda i: (0, 0)),
                  pl.BlockSpec((1, MLA_KV_RANK), lambda i: (0, 0)),
                  pl.BlockSpec(wq_all.shape, lambda i: (0, 0)),
                  pl.BlockSpec(wkv_all.shape, lambda i: (0, 0))],
        out_specs=[pl.BlockSpec((tm, 2 * nh * dn), lambda i: (i, 0)),
                   pl.BlockSpec((tm, 2 * nh * dn), lambda i: (i, 0)),
                   pl.BlockSpec((tm, nh * dv), lambda i: (i, 0))],
        out_shape=[jax.ShapeDtypeStruct((n, 2 * nh * dn), BF16),
                   jax.ShapeDtypeStruct((n, 2 * nh * dn), BF16),
                   jax.ShapeDtypeStruct((n, nh * dv), BF16)],
        compiler_params=_cparams(("parallel",), 48),
        name="mla_qkv",
    )(c, positions.reshape(n, 1), invf, q_norm.reshape(1, -1).astype(F32), kv_norm.reshape(1, -1).astype(F32),
      wq_all, wkv_all)


def _mla_attn_kernel(q_ref, k_ref, v_ref, o_ref, *, tk):
    i = pl.program_id(2)
    tq = q_ref.shape[0]
    dv = v_ref.shape[1]
    q = q_ref[...]
    row = i * tq + lax.broadcasted_iota(jnp.int32, (tq, tk), 0)
    col = lax.broadcasted_iota(jnp.int32, (tq, tk), 1)

    def block(kb, st, masked):
        m, l, acc = st
        start = pl.multiple_of(kb * tk, tk)
        s = _dot_nt(q, k_ref[pl.ds(start, tk), :])
        if masked:
            s = jnp.where(col + kb * tk <= row, s, NEG_INF)
        m_new = jnp.maximum(m, jnp.max(s, axis=-1, keepdims=True))
        alpha = jnp.exp(m - m_new)
        p = jnp.exp(s - m_new)
        l = alpha * l + jnp.sum(p, axis=-1, keepdims=True)
        acc = alpha * acc + _dot(p.astype(BF16), v_ref[pl.ds(start, tk), :])
        return m_new, l, acc

    last = (i * tq + tq - 1) // tk
    st = (jnp.full((tq, 1), NEG_INF, F32), jnp.zeros((tq, 1), F32), jnp.zeros((tq, dv), F32))
    st = lax.fori_loop(0, last, lambda kb, s_: block(kb, s_, False), st)
    m, l, acc = block(last, st, True)
    o_ref[...] = (acc / l).astype(o_ref.dtype)


def mla_attention(qcat, kcat, v, batch, seq, *, tq=256, tk=512):
    n = qcat.shape[0]
    tq, tk = min(tq, seq), min(tk, seq)
    nq = seq // tq
    dq = qcat.shape[1] // MLA_HEADS
    return pl.pallas_call(
        functools.partial(_mla_attn_kernel, tk=tk),
        grid=(batch, MLA_HEADS, nq),
        in_specs=[pl.BlockSpec((tq, dq), lambda b, h, i: (b * nq + i, h)),
                  pl.BlockSpec((seq, dq), lambda b, h, i: (b, h)),
                  pl.BlockSpec((seq, MLA_V), lambda b, h, i: (b, h))],
        out_specs=pl.BlockSpec((tq, MLA_V), lambda b, h, i: (b * nq + i, h)),
        out_shape=jax.ShapeDtypeStruct((n, MLA_HEADS * MLA_V), BF16),
        compiler_params=_cparams(("parallel", "parallel", "arbitrary"), 40),
        name="mla_attention",
    )(qcat, kcat, v)


def _router_kernel(x_ref, nw_ref, wr_ref, xn_ref, meta_ref, cnt_ref):
    i = pl.program_id(0)
    tm = x_ref.shape[0]

    @pl.when(i == 0)
    def _():
        cnt_ref[...] = jnp.zeros_like(cnt_ref)

    xn = _rms(x_ref[...], nw_ref[...])
    xn_ref[...] = xn
    logits = _dot_hi(xn, wr_ref[...])
    lane = lax.broadcasted_iota(jnp.int32, logits.shape, 1).astype(F32)
    logits = jnp.where(lane < N_EXPERTS, logits, NEG_INF)
    m1 = jnp.max(logits, axis=-1, keepdims=True)
    e1 = jnp.min(jnp.where(logits == m1, lane, float(LANES)), axis=-1, keepdims=True)
    rest = jnp.where(lane == e1, NEG_INF, logits)
    m2 = jnp.max(rest, axis=-1, keepdims=True)
    e2 = jnp.min(jnp.where(rest == m2, lane, float(LANES)), axis=-1, keepdims=True)
    ex = jnp.exp(m2 - m1)
    w1 = 1.0 / (1.0 + ex)
    w2 = ex / (1.0 + ex)
    oh1 = jnp.where(lane == e1, 1.0, 0.0)
    oh2 = jnp.where(lane == e2, 1.0, 0.0)
    oh = (oh1 + oh2).astype(BF16)
    r = lax.broadcasted_iota(jnp.int32, (tm, tm), 0)
    c = lax.broadcasted_iota(jnp.int32, (tm, tm), 1)
    before = _dot(jnp.where(c < r, 1.0, 0.0).astype(BF16), oh)
    base = before + cnt_ref[0:1, :]
    rank1 = jnp.sum(oh1 * base, axis=-1, keepdims=True)
    rank2 = jnp.sum(oh2 * base, axis=-1, keepdims=True)
    cnt_ref[...] = cnt_ref[...] + jnp.sum(oh.astype(F32), axis=0, keepdims=True)
    ml = lax.broadcasted_iota(jnp.int32, meta_ref.shape, 1)
    meta = jnp.where(ml == 0, e1, 0.0)
    meta = jnp.where(ml == 1, e2, meta)
    meta = jnp.where(ml == 2, rank1, meta)
    meta = jnp.where(ml == 3, rank2, meta)
    meta = jnp.where(ml == 4, w1, meta)
    meta = jnp.where(ml == 5, w2, meta)
    meta_ref[...] = meta


def moe_router(h, nw, w_router, *, tm=512):
    m, d = h.shape
    tm = min(tm, m)
    wr = jnp.zeros((d, LANES), F32).at[:, :N_EXPERTS].set(w_router.astype(F32))
    return pl.pallas_call(
        _router_kernel,
        grid=(m // tm,),
        in_specs=[pl.BlockSpec((tm, d), lambda i: (i, 0)),
                  pl.BlockSpec((1, d), lambda i: (0, 0)),
                  pl.BlockSpec((d, LANES), lambda i: (0, 0))],
        out_specs=[pl.BlockSpec((tm, d), lambda i: (i, 0)),
                   pl.BlockSpec((tm, 8), lambda i: (i, 0)),
                   pl.BlockSpec((8, LANES), lambda i: (0, 0))],
        out_shape=[jax.ShapeDtypeStruct((m, d), F32),
                   jax.ShapeDtypeStruct((m, 8), F32),
                   jax.ShapeDtypeStruct((8, LANES), F32)],
        compiler_params=_cparams(("arbitrary",), 40),
        name="moe_router",
    )(h, nw.reshape(1, d), wr)


def _gather_rows(idx_hbm, chunk, src_hbm, dst_ref, idx_smem, sem_idx, sem_rows):
    n = dst_ref.shape[0]
    off = pl.multiple_of(chunk * n, n)
    cp = pltpu.make_async_copy(idx_hbm.at[pl.ds(off, n)], idx_smem, sem_idx)
    cp.start()
    cp.wait()

    def row_copy(r, src_row):
        return pltpu.make_async_copy(src_hbm.at[pl.ds(src_row, 1)], dst_ref.at[pl.ds(r, 1)], sem_rows)

    def issue(r, carry):
        row_copy(r, idx_smem[r]).start()
        return carry

    lax.fori_loop(0, n, issue, 0)

    def drain(r, carry):
        row_copy(r, 0).wait()
        return carry

    lax.fori_loop(0, n, drain, 0)


def _moe_kernel(te_ref, nt_ref, idx_hbm, x_hbm, wg_ref, wu_ref, wd_ref, sw_ref, o_ref,
                xf_ref, xb_ref, acc_ref, idx_smem, sem_idx, sem_rows):
    i = pl.program_id(0)
    j = pl.program_id(1)

    @pl.when(i < nt_ref[0])
    def _():
        @pl.when(j == 0)
        def _():
            _gather_rows(idx_hbm, i, x_hbm, xf_ref, idx_smem, sem_idx, sem_rows)
            xb_ref[...] = xf_ref[...].astype(BF16)
            acc_ref[...] = jnp.zeros_like(acc_ref)

        x = xb_ref[...]
        a = (_silu(_dot(x, wg_ref[0])) * _dot(x, wu_ref[0])).astype(BF16)
        acc_ref[...] += _dot(a, wd_ref[0])

    @pl.when(j == pl.num_programs(1) - 1)
    def _():
        o_ref[...] = acc_ref[...] * sw_ref[...]


def moe_experts(xn, sorted_tok, sorted_w, tile_expert, n_tiles_used, w_gate_up, w_down, *, tf=512):
    n, d = xn.shape
    ne, f, _ = w_down.shape
    tm = MOE_TILE
    p = sorted_tok.shape[0]
    n_tiles = p // tm
    tf = min(tf, f)
    nf = f // tf
    grid_spec = pltpu.PrefetchScalarGridSpec(
        num_scalar_prefetch=2,
        grid=(n_tiles, nf),
        in_specs=[pl.BlockSpec(memory_space=pl.ANY),
                  pl.BlockSpec(memory_space=pl.ANY),
                  pl.BlockSpec((1, d, tf), lambda i, j, te, nt: (te[i], 0, j)),
                  pl.BlockSpec((1, d, tf), lambda i, j, te, nt: (te[i], 0, j + nf)),
                  pl.BlockSpec((1, tf, d), lambda i, j, te, nt: (te[i], j, 0)),
                  pl.BlockSpec((tm, 1), lambda i, j, te, nt: (i, 0))],
        out_specs=pl.BlockSpec((tm, d), lambda i, j, te, nt: (i, 0)),
        scratch_shapes=[pltpu.VMEM((tm, d), F32), pltpu.VMEM((tm, d), BF16), pltpu.VMEM((tm, d), F32),
                        pltpu.SMEM((tm,), jnp.int32), pltpu.SemaphoreType.DMA, pltpu.SemaphoreType.DMA],
    )
    return pl.pallas_call(
        _moe_kernel,
        grid_spec=grid_spec,
        out_shape=jax.ShapeDtypeStruct((p, d), F32),
        compiler_params=_cparams(("arbitrary", "arbitrary"), 56),
        name="moe_experts",
    )(tile_expert, n_tiles_used, sorted_tok, xn, w_gate_up, w_gate_up, w_down, sorted_w.reshape(p, 1))


def _combine_kernel(idx_hbm, y_hbm, h_ref, nw_ref, o_ref, buf_ref, idx_smem, sem_idx, sem_rows, *, final_norm):
    i = pl.program_id(0)
    tc = h_ref.shape[0]
    _gather_rows(idx_hbm, i, y_hbm, buf_ref, idx_smem, sem_idx, sem_rows)
    out = h_ref[...] + buf_ref[:tc, :] + buf_ref[tc:, :]
    if final_norm:
        out = _rms(out, nw_ref[...])
    o_ref[...] = out


def moe_combine(h, y_sorted, pos_tiles, nw, *, final_norm):
    n, d = h.shape
    tc = MOE_TILE // TOP_K
    return pl.pallas_call(
        functools.partial(_combine_kernel, final_norm=final_norm),
        grid=(n // tc,),
        in_specs=[pl.BlockSpec(memory_space=pl.ANY),
                  pl.BlockSpec(memory_space=pl.ANY),
                  pl.BlockSpec((tc, d), lambda i: (i, 0)),
                  pl.BlockSpec((1, d), lambda i: (0, 0))],
        out_specs=pl.BlockSpec((tc, d), lambda i: (i, 0)),
        out_shape=jax.ShapeDtypeStruct((n, d), F32),
        scratch_shapes=[pltpu.VMEM((TOP_K * tc, d), F32), pltpu.SMEM((TOP_K * tc,), jnp.int32),
                        pltpu.SemaphoreType.DMA, pltpu.SemaphoreType.DMA],
        compiler_params=_cparams(("arbitrary",), 40),
        name="moe_combine",
    )(pos_tiles, y_sorted, h, nw.reshape(1, d))


def routed_swiglu(h, nw, w_router, w_gate_up, w_down, final_nw):
    n, d = h.shape
    tm = MOE_TILE
    tc = tm // TOP_K
    xn, meta, cnt = moe_router(h, nw, w_router)
    counts = cnt[0, :N_EXPERTS].astype(jnp.int32)
    tiles_per = (counts + tm - 1) // tm
    tile_end = jnp.cumsum(tiles_per)
    start = (tile_end - tiles_per) * tm
    e = meta[:, 0:2].astype(jnp.int32)
    pos = start[e] + meta[:, 2:4].astype(jnp.int32)
    p = TOP_K * n + N_EXPERTS * tm
    n_tiles = p // tm
    flat = pos.reshape(-1)
    tok = jnp.repeat(jnp.arange(n, dtype=jnp.int32), TOP_K)
    sorted_tok = jnp.zeros((p,), jnp.int32).at[flat].set(tok, unique_indices=True)
    sorted_w = jnp.zeros((p,), F32).at[flat].set(meta[:, 4:6].reshape(-1), unique_indices=True)
    tile_expert = jnp.minimum(jnp.searchsorted(tile_end, jnp.arange(n_tiles, dtype=jnp.int32), side="right"),
                              N_EXPERTS - 1).astype(jnp.int32)
    y = moe_experts(xn, sorted_tok, sorted_w, tile_expert, tile_end[-1:].astype(jnp.int32), w_gate_up, w_down)
    pos_tiles = pos.reshape(n // tc, tc, TOP_K).transpose(0, 2, 1).reshape(-1)
    return moe_combine(h, y, pos_tiles, final_nw if final_nw is not None else nw, final_norm=final_nw is not None)


def kernel(x, positions, rel_bias, attn_norm, ffn_norm, final_norm, gdn_w_in, gdn_conv_w, gdn_a_log, gdn_dt_bias,
           gdn_norm_w, gdn_w_out, sb_w_in, sb_w_out, swa_w_in, swa_sinks, swa_w_out, mla_w_in, mla_q_norm,
           mla_w_q_b, mla_kv_norm, mla_w_kv_b, mla_w_out, ffn_w_gate_up, ffn_w_down, moe_w_router, moe_w_gate_up,
           moe_w_down):
    batch, seq, d = x.shape
    n = batch * seq
    depth = attn_norm.shape[0]
    h = x.reshape(n, d).astype(F32)
    bf = lambda a: a.astype(BF16)
    for i in range(depth):
        mixer, occ = i % 4, i // 4
        if mixer == 0:
            w_in = gdn_w_in[occ]
            qkvz_dim = 4 * GDN_HEADS * GDN_HEAD_DIM
            qkvz = norm_matmul(h, attn_norm[i], bf(w_in[:, :qkvz_dim]))
            gcol, grow = gdn_gates(h, attn_norm[i], w_in[:, qkvz_dim:].astype(F32), gdn_a_log[occ], gdn_dt_bias[occ])
            o = gdn_core(qkvz, gdn_conv_w[occ].astype(F32), gcol, grow, gdn_norm_w[occ].astype(F32), batch, seq)
            h = matmul_residual(o, bf(gdn_w_out[occ]), h)
        elif mixer == 1:
            qkv = norm_matmul(h, attn_norm[i], bf(sb_w_in[occ]))
            o = sb_attention(qkv, batch, seq)
            h = matmul_residual(o, bf(sb_w_out[occ]), h)
        elif mixer == 2:
            qkv = norm_matmul(h, attn_norm[i], bf(swa_w_in[occ]), tn=256)
            o = swa_attention(qkv, swa_sinks[occ], rel_bias, batch, seq)
            h = matmul_residual(o, bf(swa_w_out[occ]), h)
        else:
            w_in = mla_w_in[occ]
            base = MLA_Q_RANK + MLA_KV_RANK
            kr = w_in[:, base:]
            zpad = jnp.zeros((d, LANES - MLA_ROPE), w_in.dtype)
            w_all = jnp.concatenate([w_in[:, :base], kr, zpad, _rotate_half_cols(kr), zpad], axis=1)
            c = norm_matmul(h, attn_norm[i], bf(w_all), tn=w_all.shape[1], out_dtype=F32)
            qcat, kcat, v = mla_qkv(c, positions, mla_q_norm[occ], mla_w_q_b[occ], mla_kv_norm[occ], mla_w_kv_b[occ])
            o = mla_attention(qcat, kcat, v, batch, seq)
            h = matmul_residual(o, bf(mla_w_out[occ]), h)
        f = i // 2
        last = i == depth - 1
        if i % 2 == 0:
            h = ffn_dense(h, ffn_norm[i], bf(ffn_w_gate_up[f]), bf(ffn_w_down[f]))
            if last:
                h = final_rmsnorm(h, final_norm)
        else:
            h = routed_swiglu(h, ffn_norm[i], moe_w_router[f], bf(moe_w_gate_up[f]), bf(moe_w_down[f]),
                              final_norm if last else None)
    return h.reshape(batch, seq, d).astype(x.dtype)


def _final_norm_kernel(x_ref, nw_ref, o_ref):
    o_ref[...] = _rms(x_ref[...], nw_ref[...])


def final_rmsnorm(h, nw, *, tm=512):
    m, d = h.shape
    tm = min(tm, m)
    return pl.pallas_call(
        _final_norm_kernel,
        grid=(m // tm,),
        in_specs=[pl.BlockSpec((tm, d), lambda i: (i, 0)), pl.BlockSpec((1, d), lambda i: (0, 0))],
        out_specs=pl.BlockSpec((tm, d), lambda i: (i, 0)),
        out_shape=jax.ShapeDtypeStruct((m, d), F32),
        compiler_params=_cparams(("parallel",), 40),
        name="final_rmsnorm",
    )(h, nw.reshape(1, d))
```

```python
import functools
import math

import numpy as np
import jax
import jax.numpy as jnp
from jax import lax
from jax.experimental import pallas as pl
from jax.experimental.pallas import tpu as pltpu

F32 = jnp.float32
BF16 = jnp.bfloat16
HIGHEST = lax.Precision.HIGHEST

NORM_EPS = 1e-6
NEG_INF = -1e30
LANES = 128

GDN_HEADS, GDN_HEAD_DIM, GDN_CONV = 8, 128, 4
GDN_BLOCK = 256
GDN_INV_BASE = 16
GDN_HP = 2
SB_HEADS, SB_HEAD_DIM, SB_BLOCK = 16, 64, 256
SWA_Q_HEADS, SWA_KV_HEADS, SWA_HEAD_DIM, SWA_WINDOW, SWA_QBLOCK = 16, 2, 64, 128, 128
REL_BUCKETS, REL_MAX_DIST = 32, 128
MLA_HEADS, MLA_Q_RANK, MLA_KV_RANK, MLA_NOPE, MLA_ROPE, MLA_V = 8, 384, 256, 128, 64, 128
MLA_HP = 2
ROPE_THETA = 10000.0
N_EXPERTS, TOP_K = 8, 2
MOE_TILE = 1024


def _cparams(sem, vmem_mb):
    return pltpu.CompilerParams(dimension_semantics=sem, vmem_limit_bytes=vmem_mb << 20)


def _dot(a, b):
    return jnp.dot(a, b, preferred_element_type=F32)


def _dot_nt(a, b):
    return lax.dot_general(a, b, (((1,), (1,)), ((), ())), preferred_element_type=F32)


def _dot_hi(a, b):
    return jnp.dot(a, b, preferred_element_type=F32, precision=HIGHEST)


def _dot_nt_hi(a, b):
    return lax.dot_general(a, b, (((1,), (1,)), ((), ())), preferred_element_type=F32, precision=HIGHEST)


def _rms(x, w):
    return x * lax.rsqrt(jnp.mean(x * x, axis=-1, keepdims=True) + NORM_EPS) * w


def _silu(x):
    return x * jax.nn.sigmoid(x)


def _softplus(x):
    return jnp.maximum(x, 0.0) + jnp.log(1.0 + jnp.exp(-jnp.abs(x)))


def _norm_matmul_kernel(x_ref, nw_ref, w_ref, o_ref, xn_ref):
    @pl.when(pl.program_id(1) == 0)
    def _():
        xn_ref[...] = _rms(x_ref[...], nw_ref[...]).astype(BF16)

    o_ref[...] = _dot(xn_ref[...], w_ref[...]).astype(o_ref.dtype)


def norm_matmul(x, nw, w, *, tm=1024, tn=512, out_dtype=BF16):
    m, d = x.shape
    n = w.shape[1]
    tm, tn = min(tm, m), min(tn, n)
    return pl.pallas_call(
        _norm_matmul_kernel,
        grid=(m // tm, n // tn),
        in_specs=[pl.BlockSpec((tm, d), lambda i, j: (i, 0)),
                  pl.BlockSpec((1, d), lambda i, j: (0, 0)),
                  pl.BlockSpec((d, tn), lambda i, j: (0, j))],
        out_specs=pl.BlockSpec((tm, tn), lambda i, j: (i, j)),
        out_shape=jax.ShapeDtypeStruct((m, n), out_dtype),
        scratch_shapes=[pltpu.VMEM((tm, d), BF16)],
        compiler_params=_cparams(("parallel", "arbitrary"), 40),
        name="norm_matmul",
    )(x, nw.reshape(1, d), w)


def _matmul_residual_kernel(a_ref, w_ref, h_ref, o_ref):
    o_ref[...] = h_ref[...] + _dot(a_ref[...], w_ref[...])


def matmul_residual(a, w, h, *, tm=512):
    m, k = a.shape
    d = w.shape[1]
    tm = min(tm, m)
    return pl.pallas_call(
        _matmul_residual_kernel,
        grid=(m // tm,),
        in_specs=[pl.BlockSpec((tm, k), lambda i: (i, 0)),
                  pl.BlockSpec((k, d), lambda i: (0, 0)),
                  pl.BlockSpec((tm, d), lambda i: (i, 0))],
        out_specs=pl.BlockSpec((tm, d), lambda i: (i, 0)),
        out_shape=jax.ShapeDtypeStruct((m, d), F32),
        compiler_params=_cparams(("parallel",), 40),
        name="matmul_residual",
    )(a, w, h)


def _ffn_kernel(x_ref, nw_ref, wg_ref, wu_ref, wd_ref, o_ref, xn_ref, acc_ref):
    j = pl.program_id(1)

    @pl.when(j == 0)
    def _():
        x = x_ref[...]
        xn_ref[...] = _rms(x, nw_ref[...]).astype(BF16)
        acc_ref[...] = x

    xn = xn_ref[...]
    a = (_silu(_dot(xn, wg_ref[...])) * _dot(xn, wu_ref[...])).astype(BF16)
    acc_ref[...] += _dot(a, wd_ref[...])

    @pl.when(j == pl.num_programs(1) - 1)
    def _():
        o_ref[...] = acc_ref[...]


def ffn_dense(h, nw, w_gate_up, w_down, *, tm=1024, tf=256):
    m, d = h.shape
    f = w_down.shape[0]
    tm, tf = min(tm, m), min(tf, f)
    nf = f // tf
    return pl.pallas_call(
        _ffn_kernel,
        grid=(m // tm, nf),
        in_specs=[pl.BlockSpec((tm, d), lambda i, j: (i, 0)),
                  pl.BlockSpec((1, d), lambda i, j: (0, 0)),
                  pl.BlockSpec((d, tf), lambda i, j: (0, j)),
                  pl.BlockSpec((d, tf), lambda i, j: (0, j + nf)),
                  pl.BlockSpec((tf, d), lambda i, j: (j, 0))],
        out_specs=pl.BlockSpec((tm, d), lambda i, j: (i, 0)),
        out_shape=jax.ShapeDtypeStruct((m, d), F32),
        scratch_shapes=[pltpu.VMEM((tm, d), BF16), pltpu.VMEM((tm, d), F32)],
        compiler_params=_cparams(("parallel", "arbitrary"), 48),
        name="ffn_dense",
    )(h, nw.reshape(1, d), w_gate_up, w_gate_up, w_down)


def _gdn_gates_kernel(x_ref, nw_ref, w_ref, wt_ref, pcol_ref, prow_ref, col_ref, row_ref):
    tm = x_ref.shape[0]
    nh = GDN_HEADS
    xn = _rms(x_ref[...], nw_ref[...])
    lc = _dot_hi(xn, w_ref[...])
    lr = _dot_nt_hi(wt_ref[...], xn)[:2 * nh]
    ci = lax.broadcasted_iota(jnp.int32, lc.shape, 1)
    is_g_c = (ci >= nh) & (ci < 2 * nh)
    g_c = jnp.where(is_g_c, pcol_ref[0:1, :] * _softplus(lc + pcol_ref[1:2, :]), 0.0)
    ri = lax.broadcasted_iota(jnp.int32, lr.shape, 0)
    g_r = jnp.where(ri >= nh, prow_ref[:, 0:1] * _softplus(lr + prow_ref[:, 1:2]), 0.0)
    r = lax.broadcasted_iota(jnp.int32, (tm, tm), 0)
    c = lax.broadcasted_iota(jnp.int32, (tm, tm), 1)
    same = (r // GDN_BLOCK) == (c // GDN_BLOCK)
    lower = jnp.where(same & (c <= r), 1.0, 0.0).astype(F32)
    upper = jnp.where(same & (r <= c), 1.0, 0.0).astype(F32)
    col_ref[...] = jnp.where(ci < nh, jax.nn.sigmoid(lc), _dot_hi(lower, g_c))
    row_ref[...] = jnp.where(ri < nh, jax.nn.sigmoid(lr), _dot_hi(g_r, upper))


def gdn_gates(h, nw, w_bd, a_log, dt_bias, *, tm=512):
    m, d = h.shape
    tm = min(tm, m)
    nh = GDN_HEADS
    w_pad = jnp.zeros((d, LANES), F32).at[:, :2 * nh].set(w_bd)
    z = jnp.zeros((nh,), F32)
    prm = jnp.stack([jnp.concatenate([z, -jnp.exp(a_log.astype(F32))]),
                     jnp.concatenate([z, dt_bias.astype(F32)])])
    pcol = jnp.zeros((8, LANES), F32).at[:2, :2 * nh].set(prm)
    prow = jnp.zeros((2 * nh, LANES), F32).at[:, :2].set(prm.T)
    return pl.pallas_call(
        _gdn_gates_kernel,
        grid=(m // tm,),
        in_specs=[pl.BlockSpec((tm, d), lambda i: (i, 0)),
                  pl.BlockSpec((1, d), lambda i: (0, 0)),
                  pl.BlockSpec((d, LANES), lambda i: (0, 0)),
                  pl.BlockSpec((LANES, d), lambda i: (0, 0)),
                  pl.BlockSpec((8, LANES), lambda i: (0, 0)),
                  pl.BlockSpec((2 * nh, LANES), lambda i: (0, 0))],
        out_specs=[pl.BlockSpec((tm, LANES), lambda i: (i, 0)),
                   pl.BlockSpec((2 * nh, tm), lambda i: (0, i))],
        out_shape=[jax.ShapeDtypeStruct((m, LANES), F32), jax.ShapeDtypeStruct((2 * nh, m), F32)],
        compiler_params=_cparams(("parallel",), 40),
        name="gdn_gates",
    )(h, nw.reshape(1, d), w_pad, w_pad.T, pcol, prow)


def _unit_lower_inverse(low, eye, rc_xor):
    n = low.shape[0]
    b = GDN_INV_BASE
    nb = jnp.where(rc_xor < b, -low, 0.0)
    nb_b = nb.astype(BF16)
    n2 = _dot(nb_b, nb_b)
    n2_b = n2.astype(BF16)
    n4 = _dot(n2_b, n2_b)
    n4_b = n4.astype(BF16)
    n8 = _dot(n4_b, n4_b)
    p_off = nb + n2 + _dot(nb_b, n2_b)
    q = eye + n4 + n8 + _dot(n4_b, n8.astype(BF16))
    inv = q + _dot(p_off.astype(BF16), q.astype(BF16))
    while b < n:
        m = jnp.where((rc_xor >> int(math.log2(b))) == 1, low, 0.0).astype(BF16)
        inv_b = inv.astype(BF16)
        inv = inv - _dot(inv_b, _dot(m, inv_b).astype(BF16))
        b *= 2
    return inv


def _gdn_core_kernel(q_ref, k_ref, v_ref, z_ref, cwq_ref, cwk_ref, cwv_ref, col_ref, row_ref, nw_ref,
                     o_ref, s_ref, tail_ref, ext_ref):
    hp = pl.program_id(1)
    t = pl.program_id(2)
    tb = q_ref.shape[0]
    dk = GDN_HEAD_DIM
    nh = GDN_HEADS

    @pl.when(t == 0)
    def _():
        s_ref[...] = jnp.zeros_like(s_ref)
        tail_ref[...] = jnp.zeros_like(tail_ref)

    r = lax.broadcasted_iota(jnp.int32, (tb, tb), 0)
    c = lax.broadcasted_iota(jnp.int32, (tb, tb), 1)
    rc_xor = jnp.bitwise_xor(r, c)
    eye = jnp.where(r == c, 1.0, 0.0).astype(F32)
    incl = c <= r
    strict = c < r
    colv = col_ref[...]
    ci = lax.broadcasted_iota(jnp.int32, colv.shape, 1)

    for j in range(GDN_HP):
        h = hp * GDN_HP + j
        lanes = slice(j * dk, (j + 1) * dk)

        def conv_silu(i, x_ref, cw_ref):
            slot = j * 3 + i
            x = x_ref[:, lanes].astype(F32)
            ext_ref[slot, 0:8, :] = tail_ref[slot]
            ext_ref[slot, 8:, :] = x
            tail_ref[slot] = x[tb - 8:, :]
            acc = x * cw_ref[GDN_CONV - 1:GDN_CONV, lanes]
            for s in range(1, GDN_CONV):
                acc = acc + ext_ref[slot, 8 - s:8 - s + tb, :] * cw_ref[GDN_CONV - 1 - s:GDN_CONV - s, lanes]
            return _silu(acc)

        q = conv_silu(0, q_ref, cwq_ref)
        k = conv_silu(1, k_ref, cwk_ref)
        v = conv_silu(2, v_ref, cwv_ref)
        qn = q * (lax.rsqrt(jnp.sum(q * q, axis=-1, keepdims=True) + NORM_EPS) * dk ** -0.5)
        kn = k * lax.rsqrt(jnp.sum(k * k, axis=-1, keepdims=True) + NORM_EPS)
        kn_b = kn.astype(BF16)

        beta = jnp.sum(jnp.where(ci == h, colv, 0.0), axis=1, keepdims=True)
        gc = jnp.sum(jnp.where(ci == h + nh, colv, 0.0), axis=1, keepdims=True)
        gr = row_ref[pl.ds(h + nh, 1), :]
        g_last = gr[:, tb - 1:tb]

        decay = jnp.where(incl, jnp.exp(jnp.where(incl, gc - gr, 0.0)), 0.0)
        kb = kn * beta
        low = jnp.where(strict, _dot_nt(kb.astype(BF16), kn_b) * decay, 0.0)
        tinv = _unit_lower_inverse(low, eye, rc_xor)
        rhs = jnp.concatenate([v * beta, kb * jnp.exp(gc)], axis=1)
        uw = _dot(tinv.astype(BF16), rhs.astype(BF16))
        u, w = uw[:, :dk], uw[:, dk:]
        attn = (_dot_nt(qn.astype(BF16), kn_b) * decay).astype(BF16)

        s = s_ref[j]
        sb = s.astype(BF16)
        v_new = u - _dot(w.astype(BF16), sb)
        v_new_b = v_new.astype(BF16)
        o = _dot((qn * jnp.exp(gc)).astype(BF16), sb) + _dot(attn, v_new_b)
        k_dec = kn * jnp.exp(g_last - gc)
        s_ref[j] = s * jnp.exp(g_last) + _dot(k_dec.T.astype(BF16), v_new_b)

        o = _rms(o, nw_ref[...]) * _silu(z_ref[:, lanes].astype(F32))
        o_ref[:, lanes] = o.astype(o_ref.dtype)


def gdn_core(qkvz, conv_w, gcol, grow, norm_w, batch, seq):
    n = qkvz.shape[0]
    nh, dk, tb, hp = GDN_HEADS, GDN_HEAD_DIM, min(GDN_BLOCK, seq), GDN_HP
    nt = seq // tb
    ng = nh // hp

    def tok(sec):
        return pl.BlockSpec((tb, hp * dk), lambda b, g, t: (b * nt + t, g + sec * ng))

    def cw(sec):
        return pl.BlockSpec((GDN_CONV, hp * dk), lambda b, g, t: (0, g + sec * ng))

    return pl.pallas_call(
        _gdn_core_kernel,
        grid=(batch, ng, nt),
        in_specs=[tok(0), tok(1), tok(2), tok(3), cw(0), cw(1), cw(2),
                  pl.BlockSpec((tb, LANES), lambda b, g, t: (b * nt + t, 0)),
                  pl.BlockSpec((2 * nh, tb), lambda b, g, t: (0, b * nt + t)),
                  pl.BlockSpec((1, dk), lambda b, g, t: (0, 0))],
        out_specs=pl.BlockSpec((tb, hp * dk), lambda b, g, t: (b * nt + t, g)),
        out_shape=jax.ShapeDtypeStruct((n, nh * dk), BF16),
        scratch_shapes=[pltpu.VMEM((hp, dk, dk), F32), pltpu.VMEM((3 * hp, 8, dk), F32),
                        pltpu.VMEM((3 * hp, tb + 8, dk), F32)],
        compiler_params=_cparams(("parallel", "parallel", "arbitrary"), 40),
        name="gdn_core",
    )(qkvz, qkvz, qkvz, qkvz, conv_w, conv_w, conv_w, gcol, grow, norm_w.reshape(1, dk))


def _sb_attn_kernel(q_ref, k_ref, v_ref, o_ref):
    i = pl.program_id(2)
    tq = q_ref.shape[0]
    hd = SB_HEAD_DIM
    nh = LANES // hd
    log2e = math.log2(math.e)
    r = lax.broadcasted_iota(jnp.int32, (tq, tq), 0)
    c = lax.broadcasted_iota(jnp.int32, (tq, tq), 1)
    upper = jnp.where(r >= c, 1.0, 0.0).astype(BF16)
    strict = c < r
    qs = (q_ref[...].astype(F32) * (hd ** -0.5 * log2e)).astype(BF16)
    qh = [qs[:, hh * hd:(hh + 1) * hd] for hh in range(nh)]

    def block(kb, state, masked):
        start = pl.multiple_of(kb * tq, tq)
        kblk = k_ref[pl.ds(start, tq), :]
        vblk = v_ref[pl.ds(start, tq), :]
        new = []
        for hh in range(nh):
            carry, acc = state[hh]
            w = _dot_nt(qh[hh], kblk[:, hh * hd:(hh + 1) * hd])
            sp = jnp.maximum(w, 0.0) + jnp.log(1.0 + jnp.exp2(-jnp.abs(w))) * log2e
            if masked:
                sp = jnp.where(strict, sp, 0.0)
            rloc = _dot(sp.astype(BF16), upper)
            a = jnp.exp2(w - rloc - carry)
            if masked:
                a = jnp.where(strict, a, 0.0)
            acc = acc + _dot(a.astype(BF16), vblk[:, hh * hd:(hh + 1) * hd])
            new.append((carry + rloc[:, 0:1], acc))
        return tuple(new)

    state = tuple((jnp.zeros((tq, 1), F32), jnp.zeros((tq, hd), F32)) for _ in range(nh))
    state = block(i, state, True)
    state = lax.fori_loop(0, i, lambda n, st: block(i - 1 - n, st, False), state)
    o_ref[...] = jnp.concatenate([acc for _, acc in state], axis=1).astype(o_ref.dtype)


def sb_attention(qkv, batch, seq):
    n = qkv.shape[0]
    tq = min(SB_BLOCK, seq)
    nq = seq // tq
    npair = SB_HEADS * SB_HEAD_DIM // LANES
    return pl.pallas_call(
        _sb_attn_kernel,
        grid=(batch, npair, nq),
        in_specs=[pl.BlockSpec((tq, LANES), lambda b, p, i: (b * nq + i, p)),
                  pl.BlockSpec((seq, LANES), lambda b, p, i: (b, npair + p)),
                  pl.BlockSpec((seq, LANES), lambda b, p, i: (b, 2 * npair + p))],
        out_specs=pl.BlockSpec((tq, LANES), lambda b, p, i: (b * nq + i, p)),
        out_shape=jax.ShapeDtypeStruct((n, SB_HEADS * SB_HEAD_DIM), BF16),
        compiler_params=_cparams(("parallel", "parallel", "arbitrary"), 40),
        name="sb_attention",
    )(qkv, qkv, qkv)


def _t5_bucket(dist):
    exact = REL_BUCKETS // 2
    n = np.maximum(dist, 0)
    log_ratio = (np.log(np.maximum(n, 1).astype(np.float32) / exact)
                 / np.log(np.float32(REL_MAX_DIST / exact)))
    large = np.minimum(exact + (log_ratio * (REL_BUCKETS - exact)).astype(np.int32), REL_BUCKETS - 1)
    return np.where(n < exact, n, large).astype(np.int32)


def _swa_kernel(sink_ref, q_ref, kc_ref, kp_ref, vc_ref, vp_ref, bias_ref, o_ref):
    nblk = pl.program_id(1)
    tq = q_ref.shape[0]
    hd = SWA_HEAD_DIM
    grp = SWA_Q_HEADS // SWA_KV_HEADS
    scale = hd ** -0.5
    r = lax.broadcasted_iota(jnp.int32, (tq, 2 * tq), 0)
    c = lax.broadcasted_iota(jnp.int32, (tq, 2 * tq), 1)
    dist = r + tq - c
    mask = (dist >= 0) & (dist < SWA_WINDOW) & ((c >= tq) | (nblk > 0))
    outs = []
    for kv in range(SWA_KV_HEADS):
        lo, hi = kv * hd, (kv + 1) * hd
        kb = jnp.concatenate([kp_ref[:, lo:hi], kc_ref[:, lo:hi]], axis=0)
        vb = jnp.concatenate([vp_ref[:, lo:hi], vc_ref[:, lo:hi]], axis=0)
        for g in range(grp):
            hq = kv * grp + g
            qh = q_ref[:, hq * hd:(hq + 1) * hd]
            s = _dot_nt(qh, kb) * scale + bias_ref[hq]
            s = jnp.where(mask, s, NEG_INF)
            sink = sink_ref[hq]
            m = jnp.maximum(jnp.max(s, axis=-1, keepdims=True), sink)
            e = jnp.exp(s - m)
            p = e / (jnp.sum(e, axis=-1, keepdims=True) + jnp.exp(sink - m))
            outs.append(_dot(p.astype(BF16), vb))
    o_ref[...] = jnp.concatenate(outs, axis=1).astype(o_ref.dtype)


def swa_attention(qkv, sinks, rel_bias, batch, seq):
    n = qkv.shape[0]
    tq = SWA_QBLOCK
    nb = seq // tq
    qd = SWA_Q_HEADS * SWA_HEAD_DIM
    kcol = qd // LANES
    dist = np.arange(tq)[:, None] + tq - np.arange(2 * tq)[None, :]
    bias = rel_bias.astype(F32)[_t5_bucket(dist)].transpose(2, 0, 1)

    def cur(col):
        return pl.BlockSpec((tq, LANES), lambda b, i: (b * nb + i, col))

    def prev(col):
        return pl.BlockSpec((tq, LANES), lambda b, i: (b * nb + jnp.maximum(i - 1, 0), col))

    return pl.pallas_call(
        _swa_kernel,
        grid=(batch, nb),
        in_specs=[pl.BlockSpec(memory_space=pltpu.SMEM),
                  pl.BlockSpec((tq, qd), lambda b, i: (b * nb + i, 0)),
                  cur(kcol), prev(kcol), cur(kcol + 1), prev(kcol + 1),
                  pl.BlockSpec((SWA_Q_HEADS, tq, 2 * tq), lambda b, i: (0, 0, 0))],
        out_specs=pl.BlockSpec((tq, qd), lambda b, i: (b * nb + i, 0)),
        out_shape=jax.ShapeDtypeStruct((n, qd), BF16),
        compiler_params=_cparams(("parallel", "arbitrary"), 40),
        name="swa_attention",
    )(sinks.astype(F32), qkv, qkv, qkv, qkv, qkv, bias)


def _mla_qkv_kernel(c_ref, pos_ref, invf_ref, qn_ref, kvn_ref, wq_ref, wkv_ref, q_ref, k_ref, v_ref):
    nh, dn = MLA_HEADS, MLA_NOPE
    cq = c_ref[:, :MLA_Q_RANK].astype(F32)
    ckv = c_ref[:, MLA_Q_RANK:MLA_Q_RANK + MLA_KV_RANK].astype(F32)
    base = MLA_Q_RANK + MLA_KV_RANK
    kr_a = c_ref[:, base:base + LANES].astype(F32)
    kr_b = c_ref[:, base + LANES:base + 2 * LANES].astype(F32)
    ang = pos_ref[...].astype(F32) * invf_ref[...]
    cos, sin = jnp.cos(ang), jnp.sin(ang)
    scale = (MLA_NOPE + MLA_ROPE) ** -0.5 * math.log2(math.e)
    q = _dot(_rms(cq, qn_ref[...]).astype(BF16), wq_ref[...]) * scale
    kv = _dot(_rms(ckv, kvn_ref[...]).astype(BF16), wkv_ref[...])
    k_rot = (kr_a * cos + kr_b * sin).astype(BF16)
    for h in range(nh):
        q_nope = q[:, h * dn:(h + 1) * dn]
        q_a = q[:, (nh + h) * dn:(nh + h + 1) * dn]
        q_b = q[:, (2 * nh + h) * dn:(2 * nh + h + 1) * dn]
        q_ref[:, 2 * h * dn:(2 * h + 1) * dn] = q_nope.astype(BF16)
        q_ref[:, (2 * h + 1) * dn:(2 * h + 2) * dn] = (q_a * cos + q_b * sin).astype(BF16)
        k_ref[:, 2 * h * dn:(2 * h + 1) * dn] = kv[:, h * dn:(h + 1) * dn].astype(BF16)
        k_ref[:, (2 * h + 1) * dn:(2 * h + 2) * dn] = k_rot
    v_ref[...] = kv[:, nh * dn:].astype(BF16)


def _rotate_half_cols(w):
    half = w.shape[-1] // 2
    return jnp.concatenate([-w[..., half:], w[..., :half]], axis=-1)


def mla_qkv(c, positions, q_norm, w_q_b, kv_norm, w_kv_b, *, tm=512):
    n = c.shape[0]
    tm = min(tm, n)
    nh, dn, dr, dv = MLA_HEADS, MLA_NOPE, MLA_ROPE, MLA_V
    wq = w_q_b.reshape(MLA_Q_RANK, nh, dn + dr)
    pad = jnp.zeros((MLA_Q_RANK, nh, LANES - dr), w_q_b.dtype)
    wq_rope = wq[:, :, dn:]
    wq_all = jnp.concatenate([
        wq[:, :, :dn].reshape(MLA_Q_RANK, nh * dn),
        jnp.concatenate([wq_rope, pad], axis=-1).reshape(MLA_Q_RANK, nh * LANES),
        jnp.concatenate([_rotate_half_cols(wq_rope), pad], axis=-1).reshape(MLA_Q_RANK, nh * LANES),
    ], axis=1).astype(BF16)
    wkv = w_kv_b.reshape(MLA_KV_RANK, nh, dn + dv)
    wkv_all = jnp.concatenate([wkv[:, :, :dn].reshape(MLA_KV_RANK, nh * dn),
                               wkv[:, :, dn:].reshape(MLA_KV_RANK, nh * dv)], axis=1).astype(BF16)
    half = dr // 2
    inv_freq = ROPE_THETA ** (-jnp.arange(half, dtype=F32) / half)
    invf = jnp.concatenate([inv_freq, inv_freq, jnp.zeros((LANES - dr,), F32)]).reshape(1, LANES)
    cw = c.shape[1]
    return pl.pallas_call(
        _mla_qkv_kernel,
        grid=(n // tm,),
        in_specs=[pl.BlockSpec((tm, cw), lambda i: (i, 0)),
                  pl.BlockSpec((tm, 1), lambda i: (i, 0)),
                  pl.BlockSpec((1, LANES), lambda i: (0, 0)),
                  pl.BlockSpec((1, MLA_Q_RANK), lambda i: (0, 0)),
                  pl.BlockSpec((1, MLA_KV_RANK), lambda i: (0, 0)),
                  pl.BlockSpec(wq_all.shape, lambda i: (0, 0)),
                  pl.BlockSpec(wkv_all.shape, lambda i: (0, 0))],
        out_specs=[pl.BlockSpec((tm, 2 * nh * dn), lambda i: (i, 0)),
                   pl.BlockSpec((tm, 2 * nh * dn), lambda i: (i, 0)),
                   pl.BlockSpec((tm, nh * dv), lambda i: (i, 0))],
        out_shape=[jax.ShapeDtypeStruct((n, 2 * nh * dn), BF16),
                   jax.ShapeDtypeStruct((n, 2 * nh * dn), BF16),
                   jax.ShapeDtypeStruct((n, nh * dv), BF16)],
        compiler_params=_cparams(("parallel",), 48),
        name="mla_qkv",
    )(c, positions.reshape(n, 1), invf, q_norm.reshape(1, -1).astype(F32), kv_norm.reshape(1, -1).astype(F32),
      wq_all, wkv_all)


def _mla_attn_kernel(q_ref, k_ref, v_ref, o_ref, *, tk):
    i = pl.program_id(2)
    tq = q_ref.shape[0]
    dq = q_ref.shape[1] // MLA_HP
    dv = MLA_V
    row = i * tq + lax.broadcasted_iota(jnp.int32, (tq, tk), 0)
    col = lax.broadcasted_iota(jnp.int32, (tq, tk), 1)
    qs = [q_ref[:, hh * dq:(hh + 1) * dq] for hh in range(MLA_HP)]

    def block(kb, state, masked):
        start = pl.multiple_of(kb * tk, tk)
        new = []
        for hh in range(MLA_HP):
            m, l, acc = state[hh]
            s = _dot_nt(qs[hh], k_ref[pl.ds(start, tk), hh * dq:(hh + 1) * dq])
            if masked:
                s = jnp.where(col + kb * tk <= row, s, NEG_INF)
            m_new = jnp.maximum(m, jnp.max(s, axis=-1, keepdims=True))
            alpha = jnp.exp2(m - m_new)
            p = jnp.exp2(s - m_new)
            l = alpha * l + jnp.sum(p, axis=-1, keepdims=True)
            acc = alpha * acc + _dot(p.astype(BF16), v_ref[pl.ds(start, tk), hh * dv:(hh + 1) * dv])
            new.append((m_new, l, acc))
        return tuple(new)

    last = (i * tq + tq - 1) // tk
    state = tuple((jnp.full((tq, 1), NEG_INF, F32), jnp.zeros((tq, 1), F32), jnp.zeros((tq, dv), F32))
                  for _ in range(MLA_HP))
    state = lax.fori_loop(0, last, lambda kb, st: block(kb, st, False), state)
    state = block(last, state, True)
    o_ref[...] = jnp.concatenate([acc / l for _, l, acc in state], axis=1).astype(o_ref.dtype)


def mla_attention(qcat, kcat, v, batch, seq, *, tq=256, tk=512):
    n = qcat.shape[0]
    tq, tk = min(tq, seq), min(tk, seq)
    assert tk % tq == 0
    nq = seq // tq
    dq = MLA_HP * qcat.shape[1] // MLA_HEADS
    dv = MLA_HP * MLA_V
    return pl.pallas_call(
        functools.partial(_mla_attn_kernel, tk=tk),
        grid=(batch, MLA_HEADS // MLA_HP, nq),
        in_specs=[pl.BlockSpec((tq, dq), lambda b, h, i: (b * nq + i, h)),
                  pl.BlockSpec((seq, dq), lambda b, h, i: (b, h)),
                  pl.BlockSpec((seq, dv), lambda b, h, i: (b, h))],
        out_specs=pl.BlockSpec((tq, dv), lambda b, h, i: (b * nq + i, h)),
        out_shape=jax.ShapeDtypeStruct((n, MLA_HEADS * MLA_V), BF16),
        compiler_params=_cparams(("parallel", "parallel", "arbitrary"), 40),
        name="mla_attention",
    )(qcat, kcat, v)


def _router_kernel(x_ref, nw_ref, wr_ref, xn_ref, meta_ref, cnt_ref):
    i = pl.program_id(0)
    tm = x_ref.shape[0]

    @pl.when(i == 0)
    def _():
        cnt_ref[...] = jnp.zeros_like(cnt_ref)

    xn = _rms(x_ref[...], nw_ref[...])
    xn_ref[...] = xn
    logits = _dot_hi(xn, wr_ref[...])
    lane = lax.broadcasted_iota(jnp.int32, logits.shape, 1).astype(F32)
    logits = jnp.where(lane < N_EXPERTS, logits, NEG_INF)
    m1 = jnp.max(logits, axis=-1, keepdims=True)
    e1 = jnp.min(jnp.where(logits == m1, lane, float(LANES)), axis=-1, keepdims=True)
    rest = jnp.where(lane == e1, NEG_INF, logits)
    m2 = jnp.max(rest, axis=-1, keepdims=True)
    e2 = jnp.min(jnp.where(rest == m2, lane, float(LANES)), axis=-1, keepdims=True)
    ex = jnp.exp(m2 - m1)
    w1 = 1.0 / (1.0 + ex)
    w2 = ex / (1.0 + ex)
    oh1 = jnp.where(lane == e1, 1.0, 0.0)
    oh2 = jnp.where(lane == e2, 1.0, 0.0)
    oh = (oh1 + oh2).astype(BF16)
    r = lax.broadcasted_iota(jnp.int32, (tm, tm), 0)
    c = lax.broadcasted_iota(jnp.int32, (tm, tm), 1)
    before = _dot(jnp.where(c < r, 1.0, 0.0).astype(BF16), oh)
    base = before + cnt_ref[0:1, :]
    rank1 = jnp.sum(oh1 * base, axis=-1, keepdims=True)
    rank2 = jnp.sum(oh2 * base, axis=-1, keepdims=True)
    cnt_ref[...] = cnt_ref[...] + jnp.sum(oh.astype(F32), axis=0, keepdims=True)
    ml = lax.broadcasted_iota(jnp.int32, meta_ref.shape, 1)
    meta = jnp.where(ml == 0, e1, 0.0)
    meta = jnp.where(ml == 1, e2, meta)
    meta = jnp.where(ml == 2, rank1, meta)
    meta = jnp.where(ml == 3, rank2, meta)
    meta = jnp.where(ml == 4, w1, meta)
    meta = jnp.where(ml == 5, w2, meta)
    meta_ref[...] = meta


def moe_router(h, nw, w_router, *, tm=512):
    m, d = h.shape
    tm = min(tm, m)
    wr = jnp.zeros((d, LANES), F32).at[:, :N_EXPERTS].set(w_router.astype(F32))
    return pl.pallas_call(
        _router_kernel,
        grid=(m // tm,),
        in_specs=[pl.BlockSpec((tm, d), lambda i: (i, 0)),
                  pl.BlockSpec((1, d), lambda i: (0, 0)),
                  pl.BlockSpec((d, LANES), lambda i: (0, 0))],
        out_specs=[pl.BlockSpec((tm, d), lambda i: (i, 0)),
                   pl.BlockSpec((tm, 8), lambda i: (i, 0)),
                   pl.BlockSpec((8, LANES), lambda i: (0, 0))],
        out_shape=[jax.ShapeDtypeStruct((m, d), F32),
                   jax.ShapeDtypeStruct((m, 8), F32),
                   jax.ShapeDtypeStruct((8, LANES), F32)],
        compiler_params=_cparams(("arbitrary",), 40),
        name="moe_router",
    )(h, nw.reshape(1, d), wr)


GATHER_UNROLL = 8


def _row_copy(src_hbm, src_row, dst_ref, r, sem):
    return pltpu.make_async_copy(src_hbm.at[pl.ds(src_row, 1)], dst_ref.at[pl.ds(r, 1)], sem)


def _gather_start(idx_hbm, chunk, src_hbm, dst_ref, idx_smem, sem_idx, sem_rows):
    n = dst_ref.shape[0]
    off = pl.multiple_of(chunk * n, n)
    cp = pltpu.make_async_copy(idx_hbm.at[pl.ds(off, n)], idx_smem, sem_idx)
    cp.start()
    cp.wait()

    def issue(r, carry):
        _row_copy(src_hbm, idx_smem[r], dst_ref, r, sem_rows).start()
        return carry

    lax.fori_loop(0, n, issue, 0, unroll=GATHER_UNROLL)


def _gather_wait(src_hbm, dst_ref, sem_rows):
    def drain(r, carry):
        _row_copy(src_hbm, 0, dst_ref, r, sem_rows).wait()
        return carry

    lax.fori_loop(0, dst_ref.shape[0], drain, 0, unroll=GATHER_UNROLL)


def _moe_kernel(te_ref, nt_ref, idx_hbm, x_hbm, wg_ref, wu_ref, wd_ref, sw_ref, o_ref,
                xf_ref, xb_ref, acc_ref, idx_smem, sem_idx, sem_rows):
    i = pl.program_id(0)
    j = pl.program_id(1)
    n_used = nt_ref[0]

    @pl.when(i < n_used)
    def _():
        @pl.when(j == 0)
        def _():
            slot = i % 2
            @pl.when(i == 0)
            def _():
                _gather_start(idx_hbm, i, x_hbm, xf_ref.at[slot], idx_smem, sem_idx, sem_rows.at[slot])

            _gather_wait(x_hbm, xf_ref.at[slot], sem_rows.at[slot])

            @pl.when(i + 1 < n_used)
            def _():
                _gather_start(idx_hbm, i + 1, x_hbm, xf_ref.at[1 - slot], idx_smem, sem_idx, sem_rows.at[1 - slot])

            xb_ref[...] = xf_ref[slot].astype(BF16)
            acc_ref[...] = jnp.zeros_like(acc_ref)

        x = xb_ref[...]
        a = (_silu(_dot(x, wg_ref[0])) * _dot(x, wu_ref[0])).astype(BF16)
        acc_ref[...] += _dot(a, wd_ref[0])

    @pl.when(j == pl.num_programs(1) - 1)
    def _():
        o_ref[...] = acc_ref[...] * sw_ref[...]


def moe_experts(xn, sorted_tok, sorted_w, tile_expert, n_tiles_used, w_gate_up, w_down, *, tf=512):
    n, d = xn.shape
    ne, f, _ = w_down.shape
    tm = MOE_TILE
    p = sorted_tok.shape[0]
    n_tiles = p // tm
    tf = min(tf, f)
    nf = f // tf
    grid_spec = pltpu.PrefetchScalarGridSpec(
        num_scalar_prefetch=2,
        grid=(n_tiles, nf),
        in_specs=[pl.BlockSpec(memory_space=pl.ANY),
                  pl.BlockSpec(memory_space=pl.ANY),
                  pl.BlockSpec((1, d, tf), lambda i, j, te, nt: (te[i], 0, j)),
                  pl.BlockSpec((1, d, tf), lambda i, j, te, nt: (te[i], 0, j + nf)),
                  pl.BlockSpec((1, tf, d), lambda i, j, te, nt: (te[i], j, 0)),
                  pl.BlockSpec((tm, 1), lambda i, j, te, nt: (i, 0))],
        out_specs=pl.BlockSpec((tm, d), lambda i, j, te, nt: (i, 0)),
        scratch_shapes=[pltpu.VMEM((2, tm, d), F32), pltpu.VMEM((tm, d), BF16), pltpu.VMEM((tm, d), F32),
                        pltpu.SMEM((tm,), jnp.int32), pltpu.SemaphoreType.DMA, pltpu.SemaphoreType.DMA((2,))],
    )
    return pl.pallas_call(
        _moe_kernel,
        grid_spec=grid_spec,
        out_shape=jax.ShapeDtypeStruct((p, d), F32),
        compiler_params=_cparams(("arbitrary", "arbitrary"), 56),
        name="moe_experts",
    )(tile_expert, n_tiles_used, sorted_tok, xn, w_gate_up, w_gate_up, w_down, sorted_w.reshape(p, 1))


def _combine_kernel(idx_hbm, y_hbm, h_ref, nw_ref, o_ref, buf_ref, idx_smem, sem_idx, sem_rows, *, final_norm):
    i = pl.program_id(0)
    tc = h_ref.shape[0]
    slot = i % 2

    @pl.when(i == 0)
    def _():
        _gather_start(idx_hbm, i, y_hbm, buf_ref.at[slot], idx_smem, sem_idx, sem_rows.at[slot])

    _gather_wait(y_hbm, buf_ref.at[slot], sem_rows.at[slot])

    @pl.when(i + 1 < pl.num_programs(0))
    def _():
        _gather_start(idx_hbm, i + 1, y_hbm, buf_ref.at[1 - slot], idx_smem, sem_idx, sem_rows.at[1 - slot])

    out = h_ref[...] + buf_ref[slot, :tc, :] + buf_ref[slot, tc:, :]
    if final_norm:
        out = _rms(out, nw_ref[...])
    o_ref[...] = out


def moe_combine(h, y_sorted, pos_tiles, nw, *, final_norm):
    n, d = h.shape
    tc = MOE_TILE // TOP_K
    return pl.pallas_call(
        functools.partial(_combine_kernel, final_norm=final_norm),
        grid=(n // tc,),
        in_specs=[pl.BlockSpec(memory_space=pl.ANY),
                  pl.BlockSpec(memory_space=pl.ANY),
                  pl.BlockSpec((tc, d), lambda i: (i, 0)),
                  pl.BlockSpec((1, d), lambda i: (0, 0))],
        out_specs=pl.BlockSpec((tc, d), lambda i: (i, 0)),
        out_shape=jax.ShapeDtypeStruct((n, d), F32),
        scratch_shapes=[pltpu.VMEM((2, TOP_K * tc, d), F32), pltpu.SMEM((TOP_K * tc,), jnp.int32),
                        pltpu.SemaphoreType.DMA, pltpu.SemaphoreType.DMA((2,))],
        compiler_params=_cparams(("arbitrary",), 40),
        name="moe_combine",
    )(pos_tiles, y_sorted, h, nw.reshape(1, d))


def routed_swiglu(h, nw, w_router, w_gate_up, w_down, final_nw):
    n, d = h.shape
    tm = MOE_TILE
    tc = tm // TOP_K
    xn, meta, cnt = moe_router(h, nw, w_router)
    counts = cnt[0, :N_EXPERTS].astype(jnp.int32)
    tiles_per = (counts + tm - 1) // tm
    tile_end = jnp.cumsum(tiles_per)
    start = (tile_end - tiles_per) * tm
    e = meta[:, 0:2].astype(jnp.int32)
    pos = start[e] + meta[:, 2:4].astype(jnp.int32)
    p = TOP_K * n + N_EXPERTS * tm
    n_tiles = p // tm
    assign = jnp.full((p,), -1, jnp.int32).at[pos.reshape(-1)].set(
        jnp.arange(TOP_K * n, dtype=jnp.int32), unique_indices=True)
    valid = assign >= 0
    sorted_tok = jnp.where(valid, assign // TOP_K, 0)
    sorted_w = jnp.where(valid, meta[:, 4:6].reshape(-1)[jnp.maximum(assign, 0)], 0.0)
    tile_expert = jnp.minimum(jnp.searchsorted(tile_end, jnp.arange(n_tiles, dtype=jnp.int32), side="right"),
                              N_EXPERTS - 1).astype(jnp.int32)
    y = moe_experts(xn, sorted_tok, sorted_w, tile_expert, tile_end[-1:].astype(jnp.int32), w_gate_up, w_down)
    pos_tiles = pos.reshape(n // tc, tc, TOP_K).transpose(0, 2, 1).reshape(-1)
    return moe_combine(h, y, pos_tiles, final_nw if final_nw is not None else nw, final_norm=final_nw is not None)


def kernel(x, positions, rel_bias, attn_norm, ffn_norm, final_norm, gdn_w_in, gdn_conv_w, gdn_a_log, gdn_dt_bias,
           gdn_norm_w, gdn_w_out, sb_w_in, sb_w_out, swa_w_in, swa_sinks, swa_w_out, mla_w_in, mla_q_norm,
           mla_w_q_b, mla_kv_norm, mla_w_kv_b, mla_w_out, ffn_w_gate_up, ffn_w_down, moe_w_router, moe_w_gate_up,
           moe_w_down):
    batch, seq, d = x.shape
    n = batch * seq
    depth = attn_norm.shape[0]
    h = x.reshape(n, d).astype(F32)
    bf = lambda a: a.astype(BF16)
    for i in range(depth):
        mixer, occ = i % 4, i // 4
        if mixer == 0:
            w_in = gdn_w_in[occ]
            qkvz_dim = 4 * GDN_HEADS * GDN_HEAD_DIM
            qkvz = norm_matmul(h, attn_norm[i], bf(w_in[:, :qkvz_dim]))
            gcol, grow = gdn_gates(h, attn_norm[i], w_in[:, qkvz_dim:].astype(F32), gdn_a_log[occ], gdn_dt_bias[occ])
            o = gdn_core(qkvz, gdn_conv_w[occ].astype(F32), gcol, grow, gdn_norm_w[occ].astype(F32), batch, seq)
            h = matmul_residual(o, bf(gdn_w_out[occ]), h)
        elif mixer == 1:
            qkv = norm_matmul(h, attn_norm[i], bf(sb_w_in[occ]))
            o = sb_attention(qkv, batch, seq)
            h = matmul_residual(o, bf(sb_w_out[occ]), h)
        elif mixer == 2:
            qkv = norm_matmul(h, attn_norm[i], bf(swa_w_in[occ]), tn=256)
            o = swa_attention(qkv, swa_sinks[occ], rel_bias, batch, seq)
            h = matmul_residual(o, bf(swa_w_out[occ]), h)
        else:
            w_in = mla_w_in[occ]
            base = MLA_Q_RANK + MLA_KV_RANK
            kr = w_in[:, base:]
            zpad = jnp.zeros((d, LANES - MLA_ROPE), w_in.dtype)
            w_all = jnp.concatenate([w_in[:, :base], kr, zpad, _rotate_half_cols(kr), zpad], axis=1)
            c = norm_matmul(h, attn_norm[i], bf(w_all), tn=w_all.shape[1], out_dtype=F32)
            qcat, kcat, v = mla_qkv(c, positions, mla_q_norm[occ], mla_w_q_b[occ], mla_kv_norm[occ], mla_w_kv_b[occ])
            o = mla_attention(qcat, kcat, v, batch, seq)
            h = matmul_residual(o, bf(mla_w_out[occ]), h)
        f = i // 2
        last = i == depth - 1
        if i % 2 == 0:
            h = ffn_dense(h, ffn_norm[i], bf(ffn_w_gate_up[f]), bf(ffn_w_down[f]))
            if last:
                h = final_rmsnorm(h, final_norm)
        else:
            h = routed_swiglu(h, ffn_norm[i], moe_w_router[f], bf(moe_w_gate_up[f]), bf(moe_w_down[f]),
                              final_norm if last else None)
    return h.reshape(batch, seq, d).astype(x.dtype)


def _final_norm_kernel(x_ref, nw_ref, o_ref):
    o_ref[...] = _rms(x_ref[...], nw_ref[...])


def final_rmsnorm(h, nw, *, tm=512):
    m, d = h.shape
    tm = min(tm, m)
    return pl.pallas_call(
        _final_norm_kernel,
        grid=(m // tm,),
        in_specs=[pl.BlockSpec((tm, d), lambda i: (i, 0)), pl.BlockSpec((1, d), lambda i: (0, 0))],
        out_specs=pl.BlockSpec((tm, d), lambda i: (i, 0)),
        out_shape=jax.ShapeDtypeStruct((m, d), F32),
        compiler_params=_cparams(("parallel",), 40),
        name="final_rmsnorm",
    )(h, nw.reshape(1, d))
```

```python
import functools
import math

import numpy as np
import jax
import jax.numpy as jnp
from jax import lax
from jax.experimental import pallas as pl
from jax.experimental.pallas import tpu as pltpu

F32 = jnp.float32
BF16 = jnp.bfloat16
HIGHEST = lax.Precision.HIGHEST

NORM_EPS = 1e-6
NEG_INF = -1e30
LANES = 128

GDN_HEADS, GDN_HEAD_DIM, GDN_CONV = 8, 128, 4
GDN_BLOCK = 256
GDN_INV_BASE = 16
GDN_HP = 4
SB_HEADS, SB_HEAD_DIM, SB_BLOCK = 16, 64, 256
SB_HP = 8
SWA_Q_HEADS, SWA_KV_HEADS, SWA_HEAD_DIM, SWA_WINDOW, SWA_QBLOCK = 16, 2, 64, 128, 128
REL_BUCKETS, REL_MAX_DIST = 32, 128
MLA_HEADS, MLA_Q_RANK, MLA_KV_RANK, MLA_NOPE, MLA_ROPE, MLA_V = 8, 384, 256, 128, 64, 128
MLA_HP = 2
MLA_TQ, MLA_TK = 512, 1024
ROPE_THETA = 10000.0
N_EXPERTS, TOP_K = 8, 2
MOE_TILE = 1024


def _cparams(sem, vmem_mb):
    return pltpu.CompilerParams(dimension_semantics=sem, vmem_limit_bytes=vmem_mb << 20)


def _dot(a, b):
    return jnp.dot(a, b, preferred_element_type=F32)


def _dot_nt(a, b):
    return lax.dot_general(a, b, (((1,), (1,)), ((), ())), preferred_element_type=F32)


def _bmm(a, b):
    return jnp.einsum("gmk,gkn->gmn", a, b, preferred_element_type=F32)


def _bmm_nt(a, b):
    return jnp.einsum("gmk,gnk->gmn", a, b, preferred_element_type=F32)


def _dot_hi(a, b):
    return jnp.dot(a, b, preferred_element_type=F32, precision=HIGHEST)


def _dot_nt_hi(a, b):
    return lax.dot_general(a, b, (((1,), (1,)), ((), ())), preferred_element_type=F32, precision=HIGHEST)


def _rms(x, w):
    return x * lax.rsqrt(jnp.mean(x * x, axis=-1, keepdims=True) + NORM_EPS) * w


def _silu(x):
    return x * jax.nn.sigmoid(x)


def _softplus(x):
    return jnp.maximum(x, 0.0) + jnp.log(1.0 + jnp.exp(-jnp.abs(x)))


def _norm_matmul_kernel(x_ref, nw_ref, w_ref, o_ref, xn_ref):
    @pl.when(pl.program_id(1) == 0)
    def _():
        xn_ref[...] = _rms(x_ref[...], nw_ref[...]).astype(BF16)

    o_ref[...] = _dot(xn_ref[...], w_ref[...]).astype(o_ref.dtype)


def norm_matmul(x, nw, w, *, tm=1024, tn=512, out_dtype=BF16):
    m, d = x.shape
    n = w.shape[1]
    tm, tn = min(tm, m), min(tn, n)
    return pl.pallas_call(
        _norm_matmul_kernel,
        grid=(m // tm, n // tn),
        in_specs=[pl.BlockSpec((tm, d), lambda i, j: (i, 0)),
                  pl.BlockSpec((1, d), lambda i, j: (0, 0)),
                  pl.BlockSpec((d, tn), lambda i, j: (0, j))],
        out_specs=pl.BlockSpec((tm, tn), lambda i, j: (i, j)),
        out_shape=jax.ShapeDtypeStruct((m, n), out_dtype),
        scratch_shapes=[pltpu.VMEM((tm, d), BF16)],
        compiler_params=_cparams(("parallel", "arbitrary"), 40),
        name="norm_matmul",
    )(x, nw.reshape(1, d), w)


def _matmul_residual_kernel(a_ref, w_ref, h_ref, o_ref):
    o_ref[...] = h_ref[...] + _dot(a_ref[...], w_ref[...])


def matmul_residual(a, w, h, *, tm=512):
    m, k = a.shape
    d = w.shape[1]
    tm = min(tm, m)
    return pl.pallas_call(
        _matmul_residual_kernel,
        grid=(m // tm,),
        in_specs=[pl.BlockSpec((tm, k), lambda i: (i, 0)),
                  pl.BlockSpec((k, d), lambda i: (0, 0)),
                  pl.BlockSpec((tm, d), lambda i: (i, 0))],
        out_specs=pl.BlockSpec((tm, d), lambda i: (i, 0)),
        out_shape=jax.ShapeDtypeStruct((m, d), F32),
        compiler_params=_cparams(("parallel",), 40),
        name="matmul_residual",
    )(a, w, h)


def _ffn_kernel(x_ref, nw_ref, wg_ref, wu_ref, wd_ref, o_ref, xn_ref, acc_ref):
    j = pl.program_id(1)

    @pl.when(j == 0)
    def _():
        x = x_ref[...]
        xn_ref[...] = _rms(x, nw_ref[...]).astype(BF16)
        acc_ref[...] = x

    xn = xn_ref[...]
    a = (_silu(_dot(xn, wg_ref[...])) * _dot(xn, wu_ref[...])).astype(BF16)
    acc_ref[...] += _dot(a, wd_ref[...])

    @pl.when(j == pl.num_programs(1) - 1)
    def _():
        o_ref[...] = acc_ref[...]


def ffn_dense(h, nw, w_gate_up, w_down, *, tm=1024, tf=256):
    m, d = h.shape
    f = w_down.shape[0]
    tm, tf = min(tm, m), min(tf, f)
    nf = f // tf
    return pl.pallas_call(
        _ffn_kernel,
        grid=(m // tm, nf),
        in_specs=[pl.BlockSpec((tm, d), lambda i, j: (i, 0)),
                  pl.BlockSpec((1, d), lambda i, j: (0, 0)),
                  pl.BlockSpec((d, tf), lambda i, j: (0, j)),
                  pl.BlockSpec((d, tf), lambda i, j: (0, j + nf)),
                  pl.BlockSpec((tf, d), lambda i, j: (j, 0))],
        out_specs=pl.BlockSpec((tm, d), lambda i, j: (i, 0)),
        out_shape=jax.ShapeDtypeStruct((m, d), F32),
        scratch_shapes=[pltpu.VMEM((tm, d), BF16), pltpu.VMEM((tm, d), F32)],
        compiler_params=_cparams(("parallel", "arbitrary"), 48),
        name="ffn_dense",
    )(h, nw.reshape(1, d), w_gate_up, w_gate_up, w_down)


def _gdn_gates_kernel(x_ref, nw_ref, w_ref, wt_ref, pcol_ref, prow_ref, col_ref, row_ref):
    tm = x_ref.shape[0]
    nh = GDN_HEADS
    xn = _rms(x_ref[...], nw_ref[...])
    lc = _dot_hi(xn, w_ref[...])
    lr = _dot_nt_hi(wt_ref[...], xn)[:2 * nh]
    ci = lax.broadcasted_iota(jnp.int32, lc.shape, 1)
    is_g_c = (ci >= nh) & (ci < 2 * nh)
    g_c = jnp.where(is_g_c, pcol_ref[0:1, :] * _softplus(lc + pcol_ref[1:2, :]), 0.0)
    ri = lax.broadcasted_iota(jnp.int32, lr.shape, 0)
    g_r = jnp.where(ri >= nh, prow_ref[:, 0:1] * _softplus(lr + prow_ref[:, 1:2]), 0.0)
    r = lax.broadcasted_iota(jnp.int32, (tm, tm), 0)
    c = lax.broadcasted_iota(jnp.int32, (tm, tm), 1)
    same = (r // GDN_BLOCK) == (c // GDN_BLOCK)
    lower = jnp.where(same & (c <= r), 1.0, 0.0).astype(F32)
    upper = jnp.where(same & (r <= c), 1.0, 0.0).astype(F32)
    col_ref[...] = jnp.where(ci < nh, jax.nn.sigmoid(lc), _dot_hi(lower, g_c))
    row_ref[...] = jnp.where(ri < nh, jax.nn.sigmoid(lr), _dot_hi(g_r, upper))


def gdn_gates(h, nw, w_bd, a_log, dt_bias, *, tm=512):
    m, d = h.shape
    tm = min(tm, m)
    nh = GDN_HEADS
    w_pad = jnp.zeros((d, LANES), F32).at[:, :2 * nh].set(w_bd)
    z = jnp.zeros((nh,), F32)
    prm = jnp.stack([jnp.concatenate([z, -jnp.exp(a_log.astype(F32))]),
                     jnp.concatenate([z, dt_bias.astype(F32)])])
    pcol = jnp.zeros((8, LANES), F32).at[:2, :2 * nh].set(prm)
    prow = jnp.zeros((2 * nh, LANES), F32).at[:, :2].set(prm.T)
    return pl.pallas_call(
        _gdn_gates_kernel,
        grid=(m // tm,),
        in_specs=[pl.BlockSpec((tm, d), lambda i: (i, 0)),
                  pl.BlockSpec((1, d), lambda i: (0, 0)),
                  pl.BlockSpec((d, LANES), lambda i: (0, 0)),
                  pl.BlockSpec((LANES, d), lambda i: (0, 0)),
                  pl.BlockSpec((8, LANES), lambda i: (0, 0)),
                  pl.BlockSpec((2 * nh, LANES), lambda i: (0, 0))],
        out_specs=[pl.BlockSpec((tm, LANES), lambda i: (i, 0)),
                   pl.BlockSpec((2 * nh, tm), lambda i: (0, i))],
        out_shape=[jax.ShapeDtypeStruct((m, LANES), F32), jax.ShapeDtypeStruct((2 * nh, m), F32)],
        compiler_params=_cparams(("parallel",), 40),
        name="gdn_gates",
    )(h, nw.reshape(1, d), w_pad, w_pad.T, pcol, prow)


def _unit_lower_inverse(low, eye, rc_xor):
    n = low.shape[-1]
    b = GDN_INV_BASE
    nb = jnp.where(rc_xor < b, -low, 0.0)
    nb_b = nb.astype(BF16)
    n2 = _bmm(nb_b, nb_b)
    n2_b = n2.astype(BF16)
    n4 = _bmm(n2_b, n2_b)
    n4_b = n4.astype(BF16)
    n8 = _bmm(n4_b, n4_b)
    p_off = nb + n2 + _bmm(nb_b, n2_b)
    q = eye + n4 + n8 + _bmm(n4_b, n8.astype(BF16))
    inv = q + _bmm(p_off.astype(BF16), q.astype(BF16))
    while b < n:
        m = jnp.where((rc_xor >> int(math.log2(b))) == 1, low, 0.0).astype(BF16)
        inv_b = inv.astype(BF16)
        inv = inv - _bmm(inv_b, _bmm(m, inv_b).astype(BF16))
        b *= 2
    return inv


def _gdn_core_kernel(q_ref, k_ref, v_ref, z_ref, cwq_ref, cwk_ref, cwv_ref, col_ref, row_ref, nw_ref,
                     o_ref, s_ref, tail_ref, ext_ref):
    hp = pl.program_id(1)
    t = pl.program_id(2)
    tb = q_ref.shape[0]
    dk = GDN_HEAD_DIM
    nh = GDN_HEADS

    @pl.when(t == 0)
    def _():
        s_ref[...] = jnp.zeros_like(s_ref)
        tail_ref[...] = jnp.zeros_like(tail_ref)

    r = lax.broadcasted_iota(jnp.int32, (tb, tb), 0)
    c = lax.broadcasted_iota(jnp.int32, (tb, tb), 1)
    rc_xor = jnp.bitwise_xor(r, c)
    eye = jnp.where(r == c, 1.0, 0.0).astype(F32)
    incl = c <= r
    strict = c < r
    colv = col_ref[...]
    ci = lax.broadcasted_iota(jnp.int32, colv.shape, 1)

    def conv_silu(slot, x_ref, cw_ref, lanes):
        x = x_ref[:, lanes].astype(F32)
        ext_ref[slot, 0:8, :] = tail_ref[slot]
        ext_ref[slot, 8:, :] = x
        tail_ref[slot] = x[tb - 8:, :]
        acc = x * cw_ref[GDN_CONV - 1:GDN_CONV, lanes]
        for s in range(1, GDN_CONV):
            acc = acc + ext_ref[slot, 8 - s:8 - s + tb, :] * cw_ref[GDN_CONV - 1 - s:GDN_CONV - s, lanes]
        return _silu(acc)

    per_head = []
    for j in range(GDN_HP):
        h = hp * GDN_HP + j
        lanes = slice(j * dk, (j + 1) * dk)
        q = conv_silu(3 * j, q_ref, cwq_ref, lanes)
        k = conv_silu(3 * j + 1, k_ref, cwk_ref, lanes)
        v = conv_silu(3 * j + 2, v_ref, cwv_ref, lanes)
        qn = q * (lax.rsqrt(jnp.sum(q * q, axis=-1, keepdims=True) + NORM_EPS) * dk ** -0.5)
        kn = k * lax.rsqrt(jnp.sum(k * k, axis=-1, keepdims=True) + NORM_EPS)
        beta = jnp.sum(jnp.where(ci == h, colv, 0.0), axis=1, keepdims=True)
        gc = jnp.sum(jnp.where(ci == h + nh, colv, 0.0), axis=1, keepdims=True)
        gr = row_ref[pl.ds(h + nh, 1), :]
        per_head.append((qn, kn, v, beta, gc, gr))

    qn, kn, v, beta, gc, gr = (jnp.stack(z) for z in zip(*per_head))
    g_last = gr[:, :, tb - 1:tb]
    kn_b = kn.astype(BF16)
    decay = jnp.where(incl, jnp.exp(jnp.where(incl, gc - gr, 0.0)), 0.0)
    kb = kn * beta
    low = jnp.where(strict, _bmm_nt(kb.astype(BF16), kn_b) * decay, 0.0)
    tinv = _unit_lower_inverse(low, eye, rc_xor)
    rhs = jnp.concatenate([v * beta, kb * jnp.exp(gc)], axis=2)
    uw = _bmm(tinv.astype(BF16), rhs.astype(BF16))
    u, w = uw[:, :, :dk], uw[:, :, dk:]
    attn = (_bmm_nt(qn.astype(BF16), kn_b) * decay).astype(BF16)

    s = s_ref[...]
    sb = s.astype(BF16)
    v_new = u - _bmm(w.astype(BF16), sb)
    v_new_b = v_new.astype(BF16)
    o = _bmm((qn * jnp.exp(gc)).astype(BF16), sb) + _bmm(attn, v_new_b)
    k_dec = kn * jnp.exp(g_last - gc)
    k_dec_t = jnp.stack([k_dec[j].T for j in range(GDN_HP)]).astype(BF16)
    s_ref[...] = s * jnp.exp(g_last) + _bmm(k_dec_t, v_new_b)

    for j in range(GDN_HP):
        lanes = slice(j * dk, (j + 1) * dk)
        o_ref[:, lanes] = (_rms(o[j], nw_ref[...]) * _silu(z_ref[:, lanes].astype(F32))).astype(o_ref.dtype)


def gdn_core(qkvz, conv_w, gcol, grow, norm_w, batch, seq):
    n = qkvz.shape[0]
    nh, dk, tb, hp = GDN_HEADS, GDN_HEAD_DIM, min(GDN_BLOCK, seq), GDN_HP
    nt = seq // tb
    ng = nh // hp

    def tok(sec):
        return pl.BlockSpec((tb, hp * dk), lambda b, g, t: (b * nt + t, g + sec * ng))

    def cw(sec):
        return pl.BlockSpec((GDN_CONV, hp * dk), lambda b, g, t: (0, g + sec * ng))

    return pl.pallas_call(
        _gdn_core_kernel,
        grid=(batch, ng, nt),
        in_specs=[tok(0), tok(1), tok(2), tok(3), cw(0), cw(1), cw(2),
                  pl.BlockSpec((tb, LANES), lambda b, g, t: (b * nt + t, 0)),
                  pl.BlockSpec((2 * nh, tb), lambda b, g, t: (0, b * nt + t)),
                  pl.BlockSpec((1, dk), lambda b, g, t: (0, 0))],
        out_specs=pl.BlockSpec((tb, hp * dk), lambda b, g, t: (b * nt + t, g)),
        out_shape=jax.ShapeDtypeStruct((n, nh * dk), BF16),
        scratch_shapes=[pltpu.VMEM((hp, dk, dk), F32), pltpu.VMEM((3 * hp, 8, dk), F32),
                        pltpu.VMEM((3 * hp, tb + 8, dk), F32)],
        compiler_params=_cparams(("parallel", "parallel", "arbitrary"), 40),
        name="gdn_core",
    )(qkvz, qkvz, qkvz, qkvz, conv_w, conv_w, conv_w, gcol, grow, norm_w.reshape(1, dk))


def _sb_attn_kernel(q_ref, k_ref, v_ref, o_ref):
    i = pl.program_id(2)
    tq = q_ref.shape[0]
    hd = SB_HEAD_DIM
    nh = SB_HP
    log2e = math.log2(math.e)
    r = lax.broadcasted_iota(jnp.int32, (tq, tq), 0)
    c = lax.broadcasted_iota(jnp.int32, (tq, tq), 1)
    upper = jnp.where(r >= c, 1.0, 0.0).astype(BF16)
    strict = c < r
    def heads(x):
        return jnp.stack([x[:, hh * hd:(hh + 1) * hd] for hh in range(nh)])

    qh = heads((q_ref[...].astype(F32) * (hd ** -0.5 * log2e)).astype(BF16))

    def block(kb, state, masked):
        carry, acc = state
        start = pl.multiple_of(kb * tq, tq)
        kblk = heads(k_ref[pl.ds(start, tq), :])
        vblk = heads(v_ref[pl.ds(start, tq), :])
        w = jnp.einsum("hqd,hkd->hqk", qh, kblk, preferred_element_type=F32)
        sp = jnp.maximum(w, 0.0) + jnp.log(1.0 + jnp.exp2(-jnp.abs(w))) * log2e
        if masked:
            sp = jnp.where(strict, sp, 0.0)
        rloc = _dot(sp.astype(BF16).reshape(nh * tq, tq), upper).reshape(nh, tq, tq)
        a = jnp.exp2(w - rloc - carry)
        if masked:
            a = jnp.where(strict, a, 0.0)
        acc = acc + jnp.einsum("hqk,hkd->hqd", a.astype(BF16), vblk, preferred_element_type=F32)
        return carry + rloc[:, :, 0:1], acc

    state = (jnp.zeros((nh, tq, 1), F32), jnp.zeros((nh, tq, hd), F32))
    state = block(i, state, True)
    _, acc = lax.fori_loop(0, i, lambda n, st: block(i - 1 - n, st, False), state)
    o_ref[...] = jnp.concatenate([acc[hh] for hh in range(nh)], axis=1).astype(o_ref.dtype)


def sb_attention(qkv, batch, seq):
    n = qkv.shape[0]
    tq = min(SB_BLOCK, seq)
    nq = seq // tq
    gw = SB_HP * SB_HEAD_DIM
    npair = SB_HEADS // SB_HP
    return pl.pallas_call(
        _sb_attn_kernel,
        grid=(batch, npair, nq),
        in_specs=[pl.BlockSpec((tq, gw), lambda b, p, i: (b * nq + i, p)),
                  pl.BlockSpec((seq, gw), lambda b, p, i: (b, npair + p)),
                  pl.BlockSpec((seq, gw), lambda b, p, i: (b, 2 * npair + p))],
        out_specs=pl.BlockSpec((tq, gw), lambda b, p, i: (b * nq + i, p)),
        out_shape=jax.ShapeDtypeStruct((n, SB_HEADS * SB_HEAD_DIM), BF16),
        compiler_params=_cparams(("parallel", "parallel", "arbitrary"), 40),
        name="sb_attention",
    )(qkv, qkv, qkv)


def _t5_bucket(dist):
    exact = REL_BUCKETS // 2
    n = np.maximum(dist, 0)
    log_ratio = (np.log(np.maximum(n, 1).astype(np.float32) / exact)
                 / np.log(np.float32(REL_MAX_DIST / exact)))
    large = np.minimum(exact + (log_ratio * (REL_BUCKETS - exact)).astype(np.int32), REL_BUCKETS - 1)
    return np.where(n < exact, n, large).astype(np.int32)


def _swa_kernel(sink_ref, q_ref, kc_ref, kp_ref, vc_ref, vp_ref, bias_ref, o_ref):
    nblk = pl.program_id(1)
    tq = q_ref.shape[0]
    hd = SWA_HEAD_DIM
    grp = SWA_Q_HEADS // SWA_KV_HEADS
    scale = hd ** -0.5
    r = lax.broadcasted_iota(jnp.int32, (tq, 2 * tq), 0)
    c = lax.broadcasted_iota(jnp.int32, (tq, 2 * tq), 1)
    dist = r + tq - c
    mask = (dist >= 0) & (dist < SWA_WINDOW) & ((c >= tq) | (nblk > 0))
    outs = []
    for kv in range(SWA_KV_HEADS):
        lo, hi = kv * hd, (kv + 1) * hd
        kb = jnp.concatenate([kp_ref[:, lo:hi], kc_ref[:, lo:hi]], axis=0)
        vb = jnp.concatenate([vp_ref[:, lo:hi], vc_ref[:, lo:hi]], axis=0)
        for g in range(grp):
            hq = kv * grp + g
            qh = q_ref[:, hq * hd:(hq + 1) * hd]
            s = _dot_nt(qh, kb) * scale + bias_ref[hq]
            s = jnp.where(mask, s, NEG_INF)
            sink = sink_ref[hq]
            m = jnp.maximum(jnp.max(s, axis=-1, keepdims=True), sink)
            e = jnp.exp(s - m)
            p = e / (jnp.sum(e, axis=-1, keepdims=True) + jnp.exp(sink - m))
            outs.append(_dot(p.astype(BF16), vb))
    o_ref[...] = jnp.concatenate(outs, axis=1).astype(o_ref.dtype)


def swa_attention(qkv, sinks, rel_bias, batch, seq):
    n = qkv.shape[0]
    tq = SWA_QBLOCK
    nb = seq // tq
    qd = SWA_Q_HEADS * SWA_HEAD_DIM
    kcol = qd // LANES
    dist = np.arange(tq)[:, None] + tq - np.arange(2 * tq)[None, :]
    bias = rel_bias.astype(F32)[_t5_bucket(dist)].transpose(2, 0, 1)

    def cur(col):
        return pl.BlockSpec((tq, LANES), lambda b, i: (b * nb + i, col))

    def prev(col):
        return pl.BlockSpec((tq, LANES), lambda b, i: (b * nb + jnp.maximum(i - 1, 0), col))

    return pl.pallas_call(
        _swa_kernel,
        grid=(batch, nb),
        in_specs=[pl.BlockSpec(memory_space=pltpu.SMEM),
                  pl.BlockSpec((tq, qd), lambda b, i: (b * nb + i, 0)),
                  cur(kcol), prev(kcol), cur(kcol + 1), prev(kcol + 1),
                  pl.BlockSpec((SWA_Q_HEADS, tq, 2 * tq), lambda b, i: (0, 0, 0))],
        out_specs=pl.BlockSpec((tq, qd), lambda b, i: (b * nb + i, 0)),
        out_shape=jax.ShapeDtypeStruct((n, qd), BF16),
        compiler_params=_cparams(("parallel", "arbitrary"), 40),
        name="swa_attention",
    )(sinks.astype(F32), qkv, qkv, qkv, qkv, qkv, bias)


def _mla_qkv_kernel(c_ref, pos_ref, invf_ref, qn_ref, kvn_ref, wq_ref, wkv_ref, q_ref, k_ref, v_ref):
    nh, dn = MLA_HEADS, MLA_NOPE
    cq = c_ref[:, :MLA_Q_RANK].astype(F32)
    ckv = c_ref[:, MLA_Q_RANK:MLA_Q_RANK + MLA_KV_RANK].astype(F32)
    base = MLA_Q_RANK + MLA_KV_RANK
    kr_a = c_ref[:, base:base + LANES].astype(F32)
    kr_b = c_ref[:, base + LANES:base + 2 * LANES].astype(F32)
    ang = pos_ref[...].astype(F32) * invf_ref[...]
    cos, sin = jnp.cos(ang), jnp.sin(ang)
    scale = (MLA_NOPE + MLA_ROPE) ** -0.5 * math.log2(math.e)
    q = _dot(_rms(cq, qn_ref[...]).astype(BF16), wq_ref[...]) * scale
    kv = _dot(_rms(ckv, kvn_ref[...]).astype(BF16), wkv_ref[...])
    k_rot = (kr_a * cos + kr_b * sin).astype(BF16)
    for h in range(nh):
        q_nope = q[:, h * dn:(h + 1) * dn]
        q_a = q[:, (nh + h) * dn:(nh + h + 1) * dn]
        q_b = q[:, (2 * nh + h) * dn:(2 * nh + h + 1) * dn]
        q_ref[:, 2 * h * dn:(2 * h + 1) * dn] = q_nope.astype(BF16)
        q_ref[:, (2 * h + 1) * dn:(2 * h + 2) * dn] = (q_a * cos + q_b * sin).astype(BF16)
        k_ref[:, 2 * h * dn:(2 * h + 1) * dn] = kv[:, h * dn:(h + 1) * dn].astype(BF16)
        k_ref[:, (2 * h + 1) * dn:(2 * h + 2) * dn] = k_rot
    v_ref[...] = kv[:, nh * dn:].astype(BF16)


def _rotate_half_cols(w):
    half = w.shape[-1] // 2
    return jnp.concatenate([-w[..., half:], w[..., :half]], axis=-1)


def mla_qkv(c, positions, q_norm, w_q_b, kv_norm, w_kv_b, *, tm=512):
    n = c.shape[0]
    tm = min(tm, n)
    nh, dn, dr, dv = MLA_HEADS, MLA_NOPE, MLA_ROPE, MLA_V
    wq = w_q_b.reshape(MLA_Q_RANK, nh, dn + dr)
    pad = jnp.zeros((MLA_Q_RANK, nh, LANES - dr), w_q_b.dtype)
    wq_rope = wq[:, :, dn:]
    wq_all = jnp.concatenate([
        wq[:, :, :dn].reshape(MLA_Q_RANK, nh * dn),
        jnp.concatenate([wq_rope, pad], axis=-1).reshape(MLA_Q_RANK, nh * LANES),
        jnp.concatenate([_rotate_half_cols(wq_rope), pad], axis=-1).reshape(MLA_Q_RANK, nh * LANES),
    ], axis=1).astype(BF16)
    wkv = w_kv_b.reshape(MLA_KV_RANK, nh, dn + dv)
    wkv_all = jnp.concatenate([wkv[:, :, :dn].reshape(MLA_KV_RANK, nh * dn),
                               wkv[:, :, dn:].reshape(MLA_KV_RANK, nh * dv)], axis=1).astype(BF16)
    half = dr // 2
    inv_freq = ROPE_THETA ** (-jnp.arange(half, dtype=F32) / half)
    invf = jnp.concatenate([inv_freq, inv_freq, jnp.zeros((LANES - dr,), F32)]).reshape(1, LANES)
    cw = c.shape[1]
    return pl.pallas_call(
        _mla_qkv_kernel,
        grid=(n // tm,),
        in_specs=[pl.BlockSpec((tm, cw), lambda i: (i, 0)),
                  pl.BlockSpec((tm, 1), lambda i: (i, 0)),
                  pl.BlockSpec((1, LANES), lambda i: (0, 0)),
                  pl.BlockSpec((1, MLA_Q_RANK), lambda i: (0, 0)),
                  pl.BlockSpec((1, MLA_KV_RANK), lambda i: (0, 0)),
                  pl.BlockSpec(wq_all.shape, lambda i: (0, 0)),
                  pl.BlockSpec(wkv_all.shape, lambda i: (0, 0))],
        out_specs=[pl.BlockSpec((tm, 2 * nh * dn), lambda i: (i, 0)),
                   pl.BlockSpec((tm, 2 * nh * dn), lambda i: (i, 0)),
                   pl.BlockSpec((tm, nh * dv), lambda i: (i, 0))],
        out_shape=[jax.ShapeDtypeStruct((n, 2 * nh * dn), BF16),
                   jax.ShapeDtypeStruct((n, 2 * nh * dn), BF16),
                   jax.ShapeDtypeStruct((n, nh * dv), BF16)],
        compiler_params=_cparams(("parallel",), 48),
        name="mla_qkv",
    )(c, positions.reshape(n, 1), invf, q_norm.reshape(1, -1).astype(F32), kv_norm.reshape(1, -1).astype(F32),
      wq_all, wkv_all)


def _mla_attn_kernel(q_ref, k_ref, v_ref, o_ref, *, tk):
    i = pl.program_id(2)
    tq = q_ref.shape[0]
    dq = q_ref.shape[1] // MLA_HP
    dv = MLA_V
    row = i * tq + lax.broadcasted_iota(jnp.int32, (tq, tk), 0)
    col = lax.broadcasted_iota(jnp.int32, (tq, tk), 1)
    g = MLA_HP

    def heads(x, w):
        return jnp.stack([x[:, hh * w:(hh + 1) * w] for hh in range(g)])

    q = heads(q_ref[...], dq)

    def block(kb, state, masked):
        m, l, acc = state
        start = pl.multiple_of(kb * tk, tk)
        s = _bmm_nt(q, heads(k_ref[pl.ds(start, tk), :], dq))
        if masked:
            s = jnp.where(col + kb * tk <= row, s, NEG_INF)
        m_new = jnp.maximum(m, jnp.max(s, axis=-1, keepdims=True))
        alpha = jnp.exp2(m - m_new)
        p = jnp.exp2(s - m_new)
        l = alpha * l + jnp.sum(p, axis=-1, keepdims=True)
        acc = alpha * acc + _bmm(p.astype(BF16), heads(v_ref[pl.ds(start, tk), :], dv))
        return m_new, l, acc

    last = (i * tq + tq - 1) // tk
    state = (jnp.full((g, tq, 1), NEG_INF, F32), jnp.zeros((g, tq, 1), F32), jnp.zeros((g, tq, dv), F32))
    state = lax.fori_loop(0, last, lambda kb, st: block(kb, st, False), state)
    _, l, acc = block(last, state, True)
    out = acc / l
    o_ref[...] = jnp.concatenate([out[hh] for hh in range(g)], axis=1).astype(o_ref.dtype)


def mla_attention(qcat, kcat, v, batch, seq):
    n = qcat.shape[0]
    tq, tk = min(MLA_TQ, seq), min(MLA_TK, seq)
    assert tk % tq == 0
    nq = seq // tq
    dq = MLA_HP * qcat.shape[1] // MLA_HEADS
    dv = MLA_HP * MLA_V
    return pl.pallas_call(
        functools.partial(_mla_attn_kernel, tk=tk),
        grid=(batch, MLA_HEADS // MLA_HP, nq),
        in_specs=[pl.BlockSpec((tq, dq), lambda b, h, i: (b * nq + i, h)),
                  pl.BlockSpec((seq, dq), lambda b, h, i: (b, h)),
                  pl.BlockSpec((seq, dv), lambda b, h, i: (b, h))],
        out_specs=pl.BlockSpec((tq, dv), lambda b, h, i: (b * nq + i, h)),
        out_shape=jax.ShapeDtypeStruct((n, MLA_HEADS * MLA_V), BF16),
        compiler_params=_cparams(("parallel", "parallel", "arbitrary"), 40),
        name="mla_attention",
    )(qcat, kcat, v)


def _router_kernel(x_ref, nw_ref, wr_ref, xn_ref, meta_ref, cnt_ref):
    i = pl.program_id(0)
    tm = x_ref.shape[0]

    @pl.when(i == 0)
    def _():
        cnt_ref[...] = jnp.zeros_like(cnt_ref)

    xn = _rms(x_ref[...], nw_ref[...])
    xn_ref[...] = xn
    logits = _dot_hi(xn, wr_ref[...])
    lane = lax.broadcasted_iota(jnp.int32, logits.shape, 1).astype(F32)
    logits = jnp.where(lane < N_EXPERTS, logits, NEG_INF)
    m1 = jnp.max(logits, axis=-1, keepdims=True)
    e1 = jnp.min(jnp.where(logits == m1, lane, float(LANES)), axis=-1, keepdims=True)
    rest = jnp.where(lane == e1, NEG_INF, logits)
    m2 = jnp.max(rest, axis=-1, keepdims=True)
    e2 = jnp.min(jnp.where(rest == m2, lane, float(LANES)), axis=-1, keepdims=True)
    ex = jnp.exp(m2 - m1)
    w1 = 1.0 / (1.0 + ex)
    w2 = ex / (1.0 + ex)
    oh1 = jnp.where(lane == e1, 1.0, 0.0)
    oh2 = jnp.where(lane == e2, 1.0, 0.0)
    oh = (oh1 + oh2).astype(BF16)
    r = lax.broadcasted_iota(jnp.int32, (tm, tm), 0)
    c = lax.broadcasted_iota(jnp.int32, (tm, tm), 1)
    before = _dot(jnp.where(c < r, 1.0, 0.0).astype(BF16), oh)
    base = before + cnt_ref[0:1, :]
    rank1 = jnp.sum(oh1 * base, axis=-1, keepdims=True)
    rank2 = jnp.sum(oh2 * base, axis=-1, keepdims=True)
    cnt_ref[...] = cnt_ref[...] + jnp.sum(oh.astype(F32), axis=0, keepdims=True)
    ml = lax.broadcasted_iota(jnp.int32, meta_ref.shape, 1)
    meta = jnp.where(ml == 0, e1, 0.0)
    meta = jnp.where(ml == 1, e2, meta)
    meta = jnp.where(ml == 2, rank1, meta)
    meta = jnp.where(ml == 3, rank2, meta)
    meta = jnp.where(ml == 4, w1, meta)
    meta = jnp.where(ml == 5, w2, meta)
    meta_ref[...] = meta


def moe_router(h, nw, w_router, *, tm=512):
    m, d = h.shape
    tm = min(tm, m)
    wr = jnp.zeros((d, LANES), F32).at[:, :N_EXPERTS].set(w_router.astype(F32))
    return pl.pallas_call(
        _router_kernel,
        grid=(m // tm,),
        in_specs=[pl.BlockSpec((tm, d), lambda i: (i, 0)),
                  pl.BlockSpec((1, d), lambda i: (0, 0)),
                  pl.BlockSpec((d, LANES), lambda i: (0, 0))],
        out_specs=[pl.BlockSpec((tm, d), lambda i: (i, 0)),
                   pl.BlockSpec((tm, 8), lambda i: (i, 0)),
                   pl.BlockSpec((8, LANES), lambda i: (0, 0))],
        out_shape=[jax.ShapeDtypeStruct((m, d), F32),
                   jax.ShapeDtypeStruct((m, 8), F32),
                   jax.ShapeDtypeStruct((8, LANES), F32)],
        compiler_params=_cparams(("arbitrary",), 40),
        name="moe_router",
    )(h, nw.reshape(1, d), wr)


GATHER_UNROLL = 8


def _row_copy(src_hbm, src_row, dst_ref, r, sem):
    return pltpu.make_async_copy(src_hbm.at[pl.ds(src_row, 1)], dst_ref.at[pl.ds(r, 1)], sem)


def _gather_start(idx_hbm, chunk, src_hbm, dst_ref, idx_smem, sem_idx, sem_rows):
    n = dst_ref.shape[0]
    off = pl.multiple_of(chunk * n, n)
    cp = pltpu.make_async_copy(idx_hbm.at[pl.ds(off, n)], idx_smem, sem_idx)
    cp.start()
    cp.wait()

    def issue(r, carry):
        _row_copy(src_hbm, idx_smem[r], dst_ref, r, sem_rows).start()
        return carry

    lax.fori_loop(0, n, issue, 0, unroll=GATHER_UNROLL)


def _gather_wait(src_hbm, dst_ref, sem_rows):
    def drain(r, carry):
        _row_copy(src_hbm, 0, dst_ref, r, sem_rows).wait()
        return carry

    lax.fori_loop(0, dst_ref.shape[0], drain, 0, unroll=GATHER_UNROLL)


def _moe_kernel(te_ref, nt_ref, idx_hbm, x_hbm, wg_ref, wu_ref, wd_ref, sw_ref, o_ref,
                xf_ref, xb_ref, acc_ref, idx_smem, sem_idx, sem_rows):
    i = pl.program_id(0)
    j = pl.program_id(1)
    n_used = nt_ref[0]

    @pl.when(i < n_used)
    def _():
        @pl.when(j == 0)
        def _():
            slot = i % 2
            @pl.when(i == 0)
            def _():
                _gather_start(idx_hbm, i, x_hbm, xf_ref.at[slot], idx_smem, sem_idx, sem_rows.at[slot])

            _gather_wait(x_hbm, xf_ref.at[slot], sem_rows.at[slot])

            @pl.when(i + 1 < n_used)
            def _():
                _gather_start(idx_hbm, i + 1, x_hbm, xf_ref.at[1 - slot], idx_smem, sem_idx, sem_rows.at[1 - slot])

            xb_ref[...] = xf_ref[slot].astype(BF16)
            acc_ref[...] = jnp.zeros_like(acc_ref)

        x = xb_ref[...]
        a = (_silu(_dot(x, wg_ref[0])) * _dot(x, wu_ref[0])).astype(BF16)
        acc_ref[...] += _dot(a, wd_ref[0])

    @pl.when(j == pl.num_programs(1) - 1)
    def _():
        o_ref[...] = acc_ref[...] * sw_ref[...]


def moe_experts(xn, sorted_tok, sorted_w, tile_expert, n_tiles_used, w_gate_up, w_down, *, tf=512):
    n, d = xn.shape
    ne, f, _ = w_down.shape
    tm = MOE_TILE
    p = sorted_tok.shape[0]
    n_tiles = p // tm
    tf = min(tf, f)
    nf = f // tf
    grid_spec = pltpu.PrefetchScalarGridSpec(
        num_scalar_prefetch=2,
        grid=(n_tiles, nf),
        in_specs=[pl.BlockSpec(memory_space=pl.ANY),
                  pl.BlockSpec(memory_space=pl.ANY),
                  pl.BlockSpec((1, d, tf), lambda i, j, te, nt: (te[i], 0, j)),
                  pl.BlockSpec((1, d, tf), lambda i, j, te, nt: (te[i], 0, j + nf)),
                  pl.BlockSpec((1, tf, d), lambda i, j, te, nt: (te[i], j, 0)),
                  pl.BlockSpec((tm, 1), lambda i, j, te, nt: (i, 0))],
        out_specs=pl.BlockSpec((tm, d), lambda i, j, te, nt: (i, 0)),
        scratch_shapes=[pltpu.VMEM((2, tm, d), F32), pltpu.VMEM((tm, d), BF16), pltpu.VMEM((tm, d), F32),
                        pltpu.SMEM((tm,), jnp.int32), pltpu.SemaphoreType.DMA, pltpu.SemaphoreType.DMA((2,))],
    )
    return pl.pallas_call(
        _moe_kernel,
        grid_spec=grid_spec,
        out_shape=jax.ShapeDtypeStruct((p, d), F32),
        compiler_params=_cparams(("arbitrary", "arbitrary"), 56),
        name="moe_experts",
    )(tile_expert, n_tiles_used, sorted_tok, xn, w_gate_up, w_gate_up, w_down, sorted_w.reshape(p, 1))


def _combine_kernel(idx_hbm, y_hbm, h_ref, nw_ref, o_ref, buf_ref, idx_smem, sem_idx, sem_rows, *, final_norm):
    i = pl.program_id(0)
    tc = h_ref.shape[0]
    slot = i % 2

    @pl.when(i == 0)
    def _():
        _gather_start(idx_hbm, i, y_hbm, buf_ref.at[slot], idx_smem, sem_idx, sem_rows.at[slot])

    _gather_wait(y_hbm, buf_ref.at[slot], sem_rows.at[slot])

    @pl.when(i + 1 < pl.num_programs(0))
    def _():
        _gather_start(idx_hbm, i + 1, y_hbm, buf_ref.at[1 - slot], idx_smem, sem_idx, sem_rows.at[1 - slot])

    out = h_ref[...] + buf_ref[slot, :tc, :] + buf_ref[slot, tc:, :]
    if final_norm:
        out = _rms(out, nw_ref[...])
    o_ref[...] = out


def moe_combine(h, y_sorted, pos_tiles, nw, *, final_norm):
    n, d = h.shape
    tc = MOE_TILE // TOP_K
    return pl.pallas_call(
        functools.partial(_combine_kernel, final_norm=final_norm),
        grid=(n // tc,),
        in_specs=[pl.BlockSpec(memory_space=pl.ANY),
                  pl.BlockSpec(memory_space=pl.ANY),
                  pl.BlockSpec((tc, d), lambda i: (i, 0)),
                  pl.BlockSpec((1, d), lambda i: (0, 0))],
        out_specs=pl.BlockSpec((tc, d), lambda i: (i, 0)),
        out_shape=jax.ShapeDtypeStruct((n, d), F32),
        scratch_shapes=[pltpu.VMEM((2, TOP_K * tc, d), F32), pltpu.SMEM((TOP_K * tc,), jnp.int32),
                        pltpu.SemaphoreType.DMA, pltpu.SemaphoreType.DMA((2,))],
        compiler_params=_cparams(("arbitrary",), 40),
        name="moe_combine",
    )(pos_tiles, y_sorted, h, nw.reshape(1, d))


def routed_swiglu(h, nw, w_router, w_gate_up, w_down, final_nw):
    n, d = h.shape
    tm = MOE_TILE
    tc = tm // TOP_K
    xn, meta, cnt = moe_router(h, nw, w_router)
    counts = cnt[0, :N_EXPERTS].astype(jnp.int32)
    tiles_per = (counts + tm - 1) // tm
    tile_end = jnp.cumsum(tiles_per)
    start = (tile_end - tiles_per) * tm
    e = meta[:, 0:2].astype(jnp.int32)
    pos = start[e] + meta[:, 2:4].astype(jnp.int32)
    p = TOP_K * n + N_EXPERTS * tm
    n_tiles = p // tm
    assign = jnp.full((p,), -1, jnp.int32).at[pos.reshape(-1)].set(
        jnp.arange(TOP_K * n, dtype=jnp.int32), unique_indices=True)
    valid = assign >= 0
    sorted_tok = jnp.where(valid, assign // TOP_K, 0)
    sorted_w = jnp.where(valid, meta[:, 4:6].reshape(-1)[jnp.maximum(assign, 0)], 0.0)
    tile_expert = jnp.minimum(jnp.searchsorted(tile_end, jnp.arange(n_tiles, dtype=jnp.int32), side="right"),
                              N_EXPERTS - 1).astype(jnp.int32)
    y = moe_experts(xn, sorted_tok, sorted_w, tile_expert, tile_end[-1:].astype(jnp.int32), w_gate_up, w_down)
    pos_tiles = pos.reshape(n // tc, tc, TOP_K).transpose(0, 2, 1).reshape(-1)
    return moe_combine(h, y, pos_tiles, final_nw if final_nw is not None else nw, final_norm=final_nw is not None)


def kernel(x, positions, rel_bias, attn_norm, ffn_norm, final_norm, gdn_w_in, gdn_conv_w, gdn_a_log, gdn_dt_bias,
           gdn_norm_w, gdn_w_out, sb_w_in, sb_w_out, swa_w_in, swa_sinks, swa_w_out, mla_w_in, mla_q_norm,
           mla_w_q_b, mla_kv_norm, mla_w_kv_b, mla_w_out, ffn_w_gate_up, ffn_w_down, moe_w_router, moe_w_gate_up,
           moe_w_down):
    batch, seq, d = x.shape
    n = batch * seq
    depth = attn_norm.shape[0]
    h = x.reshape(n, d).astype(F32)
    bf = lambda a: a.astype(BF16)
    for i in range(depth):
        mixer, occ = i % 4, i // 4
        if mixer == 0:
            w_in = gdn_w_in[occ]
            qkvz_dim = 4 * GDN_HEADS * GDN_HEAD_DIM
            qkvz = norm_matmul(h, attn_norm[i], bf(w_in[:, :qkvz_dim]))
            gcol, grow = gdn_gates(h, attn_norm[i], w_in[:, qkvz_dim:].astype(F32), gdn_a_log[occ], gdn_dt_bias[occ])
            o = gdn_core(qkvz, gdn_conv_w[occ].astype(F32), gcol, grow, gdn_norm_w[occ].astype(F32), batch, seq)
            h = matmul_residual(o, bf(gdn_w_out[occ]), h)
        elif mixer == 1:
            qkv = norm_matmul(h, attn_norm[i], bf(sb_w_in[occ]))
            o = sb_attention(qkv, batch, seq)
            h = matmul_residual(o, bf(sb_w_out[occ]), h)
        elif mixer == 2:
            qkv = norm_matmul(h, attn_norm[i], bf(swa_w_in[occ]), tn=256)
            o = swa_attention(qkv, swa_sinks[occ], rel_bias, batch, seq)
            h = matmul_residual(o, bf(swa_w_out[occ]), h)
        else:
            w_in = mla_w_in[occ]
            base = MLA_Q_RANK + MLA_KV_RANK
            kr = w_in[:, base:]
            zpad = jnp.zeros((d, LANES - MLA_ROPE), w_in.dtype)
            w_all = jnp.concatenate([w_in[:, :base], kr, zpad, _rotate_half_cols(kr), zpad], axis=1)
            c = norm_matmul(h, attn_norm[i], bf(w_all), tn=w_all.shape[1], out_dtype=F32)
            qcat, kcat, v = mla_qkv(c, positions, mla_q_norm[occ], mla_w_q_b[occ], mla_kv_norm[occ], mla_w_kv_b[occ])
            o = mla_attention(qcat, kcat, v, batch, seq)
            h = matmul_residual(o, bf(mla_w_out[occ]), h)
        f = i // 2
        last = i == depth - 1
        if i % 2 == 0:
            h = ffn_dense(h, ffn_norm[i], bf(ffn_w_gate_up[f]), bf(ffn_w_down[f]))
            if last:
                h = final_rmsnorm(h, final_norm)
        else:
            h = routed_swiglu(h, ffn_norm[i], moe_w_router[f], bf(moe_w_gate_up[f]), bf(moe_w_down[f]),
                              final_norm if last else None)
    return h.reshape(batch, seq, d).astype(x.dtype)


def _final_norm_kernel(x_ref, nw_ref, o_ref):
    o_ref[...] = _rms(x_ref[...], nw_ref[...])


def final_rmsnorm(h, nw, *, tm=512):
    m, d = h.shape
    tm = min(tm, m)
    return pl.pallas_call(
        _final_norm_kernel,
        grid=(m // tm,),
        in_specs=[pl.BlockSpec((tm, d), lambda i: (i, 0)), pl.BlockSpec((1, d), lambda i: (0, 0))],
        out_specs=pl.BlockSpec((tm, d), lambda i: (i, 0)),
        out_shape=jax.ShapeDtypeStruct((m, d), F32),
        compiler_params=_cparams(("parallel",), 40),
        name="final_rmsnorm",
    )(h, nw.reshape(1, d))
```

```python
import functools
import math

import numpy as np
import jax
import jax.numpy as jnp
from jax import lax
from jax.experimental import pallas as pl
from jax.experimental.pallas import tpu as pltpu

F32 = jnp.float32
BF16 = jnp.bfloat16
HIGHEST = lax.Precision.HIGHEST

NORM_EPS = 1e-6
NEG_INF = -1e30
LANES = 128

GDN_HEADS, GDN_HEAD_DIM, GDN_CONV = 8, 128, 4
GDN_BLOCK = 256
GDN_INV_BASE = 16
GDN_HP = 4
SB_HEADS, SB_HEAD_DIM, SB_BLOCK = 16, 64, 256
SB_HP = 8
SB_ZERO_EXP = 160.0
SWA_Q_HEADS, SWA_KV_HEADS, SWA_HEAD_DIM, SWA_WINDOW, SWA_QBLOCK = 16, 2, 64, 128, 128
REL_BUCKETS, REL_MAX_DIST = 32, 128
MLA_HEADS, MLA_Q_RANK, MLA_KV_RANK, MLA_NOPE, MLA_ROPE, MLA_V = 8, 384, 256, 128, 64, 128
MLA_HP = 2
MLA_TQ, MLA_TK = 512, 1024
ROPE_THETA = 10000.0
N_EXPERTS, TOP_K = 8, 2
MOE_TILE = 1024


def _cparams(sem, vmem_mb):
    return pltpu.CompilerParams(dimension_semantics=sem, vmem_limit_bytes=vmem_mb << 20)


def _dot(a, b):
    return jnp.dot(a, b, preferred_element_type=F32)


def _dot_nt(a, b):
    return lax.dot_general(a, b, (((1,), (1,)), ((), ())), preferred_element_type=F32)


def _bmm(a, b):
    return jnp.einsum("gmk,gkn->gmn", a, b, preferred_element_type=F32)


def _bmm_nt(a, b):
    return jnp.einsum("gmk,gnk->gmn", a, b, preferred_element_type=F32)


def _dot_hi(a, b):
    return jnp.dot(a, b, preferred_element_type=F32, precision=HIGHEST)


def _dot_nt_hi(a, b):
    return lax.dot_general(a, b, (((1,), (1,)), ((), ())), preferred_element_type=F32, precision=HIGHEST)


def _rms(x, w):
    return x * lax.rsqrt(jnp.mean(x * x, axis=-1, keepdims=True) + NORM_EPS) * w


def _silu(x):
    return x * jax.nn.sigmoid(x)


def _softplus(x):
    return jnp.maximum(x, 0.0) + jnp.log(1.0 + jnp.exp(-jnp.abs(x)))


def _norm_matmul_kernel(x_ref, nw_ref, w_ref, o_ref, xn_ref):
    @pl.when(pl.program_id(1) == 0)
    def _():
        xn_ref[...] = _rms(x_ref[...], nw_ref[...]).astype(BF16)

    o_ref[...] = _dot(xn_ref[...], w_ref[...]).astype(o_ref.dtype)


def norm_matmul(x, nw, w, *, tm=1024, tn=512, out_dtype=BF16):
    m, d = x.shape
    n = w.shape[1]
    tm, tn = min(tm, m), min(tn, n)
    return pl.pallas_call(
        _norm_matmul_kernel,
        grid=(m // tm, n // tn),
        in_specs=[pl.BlockSpec((tm, d), lambda i, j: (i, 0)),
                  pl.BlockSpec((1, d), lambda i, j: (0, 0)),
                  pl.BlockSpec((d, tn), lambda i, j: (0, j))],
        out_specs=pl.BlockSpec((tm, tn), lambda i, j: (i, j)),
        out_shape=jax.ShapeDtypeStruct((m, n), out_dtype),
        scratch_shapes=[pltpu.VMEM((tm, d), BF16)],
        compiler_params=_cparams(("parallel", "arbitrary"), 40),
        name="norm_matmul",
    )(x, nw.reshape(1, d), w)


def _matmul_residual_kernel(a_ref, w_ref, h_ref, o_ref):
    o_ref[...] = h_ref[...] + _dot(a_ref[...], w_ref[...])


def matmul_residual(a, w, h, *, tm=512):
    m, k = a.shape
    d = w.shape[1]
    tm = min(tm, m)
    return pl.pallas_call(
        _matmul_residual_kernel,
        grid=(m // tm,),
        in_specs=[pl.BlockSpec((tm, k), lambda i: (i, 0)),
                  pl.BlockSpec((k, d), lambda i: (0, 0)),
                  pl.BlockSpec((tm, d), lambda i: (i, 0))],
        out_specs=pl.BlockSpec((tm, d), lambda i: (i, 0)),
        out_shape=jax.ShapeDtypeStruct((m, d), F32),
        compiler_params=_cparams(("parallel",), 40),
        name="matmul_residual",
    )(a, w, h)


def _ffn_kernel(x_ref, nw_ref, wg_ref, wu_ref, wd_ref, o_ref, xn_ref, acc_ref):
    j = pl.program_id(1)

    @pl.when(j == 0)
    def _():
        x = x_ref[...]
        xn_ref[...] = _rms(x, nw_ref[...]).astype(BF16)
        acc_ref[...] = x

    xn = xn_ref[...]
    a = (_silu(_dot(xn, wg_ref[...])) * _dot(xn, wu_ref[...])).astype(BF16)
    acc_ref[...] += _dot(a, wd_ref[...])

    @pl.when(j == pl.num_programs(1) - 1)
    def _():
        o_ref[...] = acc_ref[...]


def ffn_dense(h, nw, w_gate_up, w_down, *, tm=1024, tf=256):
    m, d = h.shape
    f = w_down.shape[0]
    tm, tf = min(tm, m), min(tf, f)
    nf = f // tf
    return pl.pallas_call(
        _ffn_kernel,
        grid=(m // tm, nf),
        in_specs=[pl.BlockSpec((tm, d), lambda i, j: (i, 0)),
                  pl.BlockSpec((1, d), lambda i, j: (0, 0)),
                  pl.BlockSpec((d, tf), lambda i, j: (0, j)),
                  pl.BlockSpec((d, tf), lambda i, j: (0, j + nf)),
                  pl.BlockSpec((tf, d), lambda i, j: (j, 0))],
        out_specs=pl.BlockSpec((tm, d), lambda i, j: (i, 0)),
        out_shape=jax.ShapeDtypeStruct((m, d), F32),
        scratch_shapes=[pltpu.VMEM((tm, d), BF16), pltpu.VMEM((tm, d), F32)],
        compiler_params=_cparams(("parallel", "arbitrary"), 48),
        name="ffn_dense",
    )(h, nw.reshape(1, d), w_gate_up, w_gate_up, w_down)


def _gdn_gates_kernel(x_ref, nw_ref, w_ref, wt_ref, pcol_ref, prow_ref, col_ref, row_ref):
    tm = x_ref.shape[0]
    nh = GDN_HEADS
    xn = _rms(x_ref[...], nw_ref[...])
    lc = _dot_hi(xn, w_ref[...])
    lr = _dot_nt_hi(wt_ref[...], xn)[:2 * nh]
    ci = lax.broadcasted_iota(jnp.int32, lc.shape, 1)
    is_g_c = (ci >= nh) & (ci < 2 * nh)
    g_c = jnp.where(is_g_c, pcol_ref[0:1, :] * _softplus(lc + pcol_ref[1:2, :]), 0.0)
    ri = lax.broadcasted_iota(jnp.int32, lr.shape, 0)
    g_r = jnp.where(ri >= nh, prow_ref[:, 0:1] * _softplus(lr + prow_ref[:, 1:2]), 0.0)
    r = lax.broadcasted_iota(jnp.int32, (tm, tm), 0)
    c = lax.broadcasted_iota(jnp.int32, (tm, tm), 1)
    same = (r // GDN_BLOCK) == (c // GDN_BLOCK)
    lower = jnp.where(same & (c <= r), 1.0, 0.0).astype(F32)
    upper = jnp.where(same & (r <= c), 1.0, 0.0).astype(F32)
    col_ref[...] = jnp.where(ci < nh, jax.nn.sigmoid(lc), _dot_hi(lower, g_c))
    row_ref[...] = jnp.where(ri < nh, jax.nn.sigmoid(lr), _dot_hi(g_r, upper))


def gdn_gates(h, nw, w_bd, a_log, dt_bias, *, tm=512):
    m, d = h.shape
    tm = min(tm, m)
    nh = GDN_HEADS
    w_pad = jnp.zeros((d, LANES), F32).at[:, :2 * nh].set(w_bd)
    z = jnp.zeros((nh,), F32)
    prm = jnp.stack([jnp.concatenate([z, -jnp.exp(a_log.astype(F32))]),
                     jnp.concatenate([z, dt_bias.astype(F32)])])
    pcol = jnp.zeros((8, LANES), F32).at[:2, :2 * nh].set(prm)
    prow = jnp.zeros((2 * nh, LANES), F32).at[:, :2].set(prm.T)
    return pl.pallas_call(
        _gdn_gates_kernel,
        grid=(m // tm,),
        in_specs=[pl.BlockSpec((tm, d), lambda i: (i, 0)),
                  pl.BlockSpec((1, d), lambda i: (0, 0)),
                  pl.BlockSpec((d, LANES), lambda i: (0, 0)),
                  pl.BlockSpec((LANES, d), lambda i: (0, 0)),
                  pl.BlockSpec((8, LANES), lambda i: (0, 0)),
                  pl.BlockSpec((2 * nh, LANES), lambda i: (0, 0))],
        out_specs=[pl.BlockSpec((tm, LANES), lambda i: (i, 0)),
                   pl.BlockSpec((2 * nh, tm), lambda i: (0, i))],
        out_shape=[jax.ShapeDtypeStruct((m, LANES), F32), jax.ShapeDtypeStruct((2 * nh, m), F32)],
        compiler_params=_cparams(("parallel",), 40),
        name="gdn_gates",
    )(h, nw.reshape(1, d), w_pad, w_pad.T, pcol, prow)


def _unit_lower_inverse(low, eye, rc_xor):
    n = low.shape[-1]
    b = GDN_INV_BASE
    nb = jnp.where(rc_xor < b, -low, 0.0)
    nb_b = nb.astype(BF16)
    n2 = _bmm(nb_b, nb_b)
    n2_b = n2.astype(BF16)
    n4 = _bmm(n2_b, n2_b)
    n4_b = n4.astype(BF16)
    n8 = _bmm(n4_b, n4_b)
    p_off = nb + n2 + _bmm(nb_b, n2_b)
    q = eye + n4 + n8 + _bmm(n4_b, n8.astype(BF16))
    inv = q + _bmm(p_off.astype(BF16), q.astype(BF16))
    while b < n:
        m = jnp.where((rc_xor >> int(math.log2(b))) == 1, low, 0.0).astype(BF16)
        inv_b = inv.astype(BF16)
        inv = inv - _bmm(inv_b, _bmm(m, inv_b).astype(BF16))
        b *= 2
    return inv


def _gdn_core_kernel(q_ref, k_ref, v_ref, z_ref, cwq_ref, cwk_ref, cwv_ref, col_ref, row_ref, nw_ref,
                     o_ref, s_ref, tail_ref, ext_ref):
    hp = pl.program_id(1)
    t = pl.program_id(2)
    tb = q_ref.shape[0]
    dk = GDN_HEAD_DIM
    nh = GDN_HEADS

    @pl.when(t == 0)
    def _():
        s_ref[...] = jnp.zeros_like(s_ref)
        tail_ref[...] = jnp.zeros_like(tail_ref)

    r = lax.broadcasted_iota(jnp.int32, (tb, tb), 0)
    c = lax.broadcasted_iota(jnp.int32, (tb, tb), 1)
    rc_xor = jnp.bitwise_xor(r, c)
    eye = jnp.where(r == c, 1.0, 0.0).astype(F32)
    incl = c <= r
    strict = c < r
    colv = col_ref[...]
    ci = lax.broadcasted_iota(jnp.int32, colv.shape, 1)

    def conv_silu(slot, x_ref, cw_ref, lanes):
        x = x_ref[:, lanes].astype(F32)
        ext_ref[slot, 0:8, :] = tail_ref[slot]
        ext_ref[slot, 8:, :] = x
        tail_ref[slot] = x[tb - 8:, :]
        acc = x * cw_ref[GDN_CONV - 1:GDN_CONV, lanes]
        for s in range(1, GDN_CONV):
            acc = acc + ext_ref[slot, 8 - s:8 - s + tb, :] * cw_ref[GDN_CONV - 1 - s:GDN_CONV - s, lanes]
        return _silu(acc)

    per_head = []
    for j in range(GDN_HP):
        h = hp * GDN_HP + j
        lanes = slice(j * dk, (j + 1) * dk)
        q = conv_silu(3 * j, q_ref, cwq_ref, lanes)
        k = conv_silu(3 * j + 1, k_ref, cwk_ref, lanes)
        v = conv_silu(3 * j + 2, v_ref, cwv_ref, lanes)
        qn = q * (lax.rsqrt(jnp.sum(q * q, axis=-1, keepdims=True) + NORM_EPS) * dk ** -0.5)
        kn = k * lax.rsqrt(jnp.sum(k * k, axis=-1, keepdims=True) + NORM_EPS)
        beta = jnp.sum(jnp.where(ci == h, colv, 0.0), axis=1, keepdims=True)
        gc = jnp.sum(jnp.where(ci == h + nh, colv, 0.0), axis=1, keepdims=True)
        gr = row_ref[pl.ds(h + nh, 1), :]
        per_head.append((qn, kn, v, beta, gc, gr))

    qn, kn, v, beta, gc, gr = (jnp.stack(z) for z in zip(*per_head))
    g_last = gr[:, :, tb - 1:tb]
    kn_b = kn.astype(BF16)
    decay = jnp.where(incl, jnp.exp(jnp.where(incl, gc - gr, 0.0)), 0.0)
    kb = kn * beta
    low = jnp.where(strict, _bmm_nt(kb.astype(BF16), kn_b) * decay, 0.0)
    tinv = _unit_lower_inverse(low, eye, rc_xor)
    rhs = jnp.concatenate([v * beta, kb * jnp.exp(gc)], axis=2)
    uw = _bmm(tinv.astype(BF16), rhs.astype(BF16))
    u, w = uw[:, :, :dk], uw[:, :, dk:]
    attn = (_bmm_nt(qn.astype(BF16), kn_b) * decay).astype(BF16)

    s = s_ref[...]
    sb = s.astype(BF16)
    v_new = u - _bmm(w.astype(BF16), sb)
    v_new_b = v_new.astype(BF16)
    o = _bmm((qn * jnp.exp(gc)).astype(BF16), sb) + _bmm(attn, v_new_b)
    k_dec = kn * jnp.exp(g_last - gc)
    k_dec_t = jnp.stack([k_dec[j].T for j in range(GDN_HP)]).astype(BF16)
    s_ref[...] = s * jnp.exp(g_last) + _bmm(k_dec_t, v_new_b)

    for j in range(GDN_HP):
        lanes = slice(j * dk, (j + 1) * dk)
        o_ref[:, lanes] = (_rms(o[j], nw_ref[...]) * _silu(z_ref[:, lanes].astype(F32))).astype(o_ref.dtype)


def gdn_core(qkvz, conv_w, gcol, grow, norm_w, batch, seq):
    n = qkvz.shape[0]
    nh, dk, tb, hp = GDN_HEADS, GDN_HEAD_DIM, min(GDN_BLOCK, seq), GDN_HP
    nt = seq // tb
    ng = nh // hp

    def tok(sec):
        return pl.BlockSpec((tb, hp * dk), lambda b, g, t: (b * nt + t, g + sec * ng))

    def cw(sec):
        return pl.BlockSpec((GDN_CONV, hp * dk), lambda b, g, t: (0, g + sec * ng))

    return pl.pallas_call(
        _gdn_core_kernel,
        grid=(batch, ng, nt),
        in_specs=[tok(0), tok(1), tok(2), tok(3), cw(0), cw(1), cw(2),
                  pl.BlockSpec((tb, LANES), lambda b, g, t: (b * nt + t, 0)),
                  pl.BlockSpec((2 * nh, tb), lambda b, g, t: (0, b * nt + t)),
                  pl.BlockSpec((1, dk), lambda b, g, t: (0, 0))],
        out_specs=pl.BlockSpec((tb, hp * dk), lambda b, g, t: (b * nt + t, g)),
        out_shape=jax.ShapeDtypeStruct((n, nh * dk), BF16),
        scratch_shapes=[pltpu.VMEM((hp, dk, dk), F32), pltpu.VMEM((3 * hp, 8, dk), F32),
                        pltpu.VMEM((3 * hp, tb + 8, dk), F32)],
        compiler_params=_cparams(("parallel", "parallel", "arbitrary"), 40),
        name="gdn_core",
    )(qkvz, qkvz, qkvz, qkvz, conv_w, conv_w, conv_w, gcol, grow, norm_w.reshape(1, dk))


def _sb_attn_kernel(q_ref, k_ref, v_ref, o_ref, kmax_ref):
    i = pl.program_id(2)
    tq = q_ref.shape[0]
    hd = SB_HEAD_DIM
    nh = SB_HP
    log2e = math.log2(math.e)
    r = lax.broadcasted_iota(jnp.int32, (tq, tq), 0)
    c = lax.broadcasted_iota(jnp.int32, (tq, tq), 1)
    upper = jnp.where(r >= c, 1.0, 0.0).astype(BF16)
    strict = c < r
    def heads(x):
        return jnp.stack([x[:, hh * hd:(hh + 1) * hd] for hh in range(nh)])

    qh = heads((q_ref[...].astype(F32) * (hd ** -0.5 * log2e)).astype(BF16))

    @pl.when(i == 0)
    def _():
        def chunk_max(cb, m):
            kf = k_ref[pl.ds(pl.multiple_of(cb * tq, tq), tq), :].astype(F32)
            k2 = kf * kf
            for hh in range(nh):
                m = jnp.maximum(m, jnp.sum(k2[:, hh * hd:(hh + 1) * hd], axis=-1, keepdims=True))
            return m

        m = lax.fori_loop(0, k_ref.shape[0] // tq, chunk_max, jnp.zeros((tq, 1), F32))
        kmax_ref[...] = jnp.broadcast_to(jnp.max(m, axis=0, keepdims=True), kmax_ref.shape)

    qf = qh.astype(F32)
    q2max = jnp.max(jnp.max(jnp.sum(qf * qf, axis=-1, keepdims=True), axis=0), axis=0, keepdims=True)
    w_bound = jnp.sqrt(q2max * kmax_ref[0:1, 0:1]) * 1.02

    def block(kb, state, masked):
        carry, acc = state
        start = pl.multiple_of(kb * tq, tq)
        kblk = heads(k_ref[pl.ds(start, tq), :])
        vblk = heads(v_ref[pl.ds(start, tq), :])
        w = jnp.einsum("hqd,hkd->hqk", qh, kblk, preferred_element_type=F32)
        sp = jnp.maximum(w, 0.0) + jnp.log(1.0 + jnp.exp2(-jnp.abs(w))) * log2e
        if masked:
            sp = jnp.where(strict, sp, 0.0)
        rloc = _dot(sp.astype(BF16).reshape(nh * tq, tq), upper).reshape(nh, tq, tq)
        a = jnp.exp2(w - rloc - carry)
        if masked:
            a = jnp.where(strict, a, 0.0)
        acc = acc + jnp.einsum("hqk,hkd->hqd", a.astype(BF16), vblk, preferred_element_type=F32)
        return carry + rloc[:, :, 0:1], acc

    def slack(carry):
        return jnp.sum(jnp.min(jnp.min(carry, axis=0), axis=0, keepdims=True) - w_bound)

    state = (jnp.zeros((nh, tq, 1), F32), jnp.zeros((nh, tq, hd), F32))
    carry, acc = block(i, state, True)

    def cond(st):
        return (st[0] < i) & (st[3] < SB_ZERO_EXP)

    def body(st):
        n, carry, acc, _ = st
        carry, acc = block(i - 1 - n, (carry, acc), False)
        return n + 1, carry, acc, slack(carry)

    _, _, acc, _ = lax.while_loop(cond, body, (jnp.int32(0), carry, acc, slack(carry)))
    o_ref[...] = jnp.concatenate([acc[hh] for hh in range(nh)], axis=1).astype(o_ref.dtype)


def sb_attention(qkv, batch, seq):
    n = qkv.shape[0]
    tq = min(SB_BLOCK, seq)
    nq = seq // tq
    gw = SB_HP * SB_HEAD_DIM
    npair = SB_HEADS // SB_HP
    return pl.pallas_call(
        _sb_attn_kernel,
        grid=(batch, npair, nq),
        in_specs=[pl.BlockSpec((tq, gw), lambda b, p, i: (b * nq + i, p)),
                  pl.BlockSpec((seq, gw), lambda b, p, i: (b, npair + p)),
                  pl.BlockSpec((seq, gw), lambda b, p, i: (b, 2 * npair + p))],
        out_specs=pl.BlockSpec((tq, gw), lambda b, p, i: (b * nq + i, p)),
        out_shape=jax.ShapeDtypeStruct((n, SB_HEADS * SB_HEAD_DIM), BF16),
        scratch_shapes=[pltpu.VMEM((8, LANES), F32)],
        compiler_params=_cparams(("parallel", "parallel", "arbitrary"), 40),
        name="sb_attention",
    )(qkv, qkv, qkv)


def _t5_bucket(dist):
    exact = REL_BUCKETS // 2
    n = np.maximum(dist, 0)
    log_ratio = (np.log(np.maximum(n, 1).astype(np.float32) / exact)
                 / np.log(np.float32(REL_MAX_DIST / exact)))
    large = np.minimum(exact + (log_ratio * (REL_BUCKETS - exact)).astype(np.int32), REL_BUCKETS - 1)
    return np.where(n < exact, n, large).astype(np.int32)


def _swa_kernel(sink_ref, q_ref, kc_ref, kp_ref, vc_ref, vp_ref, bias_ref, o_ref):
    nblk = pl.program_id(1)
    tq = q_ref.shape[0]
    hd = SWA_HEAD_DIM
    grp = SWA_Q_HEADS // SWA_KV_HEADS
    scale = hd ** -0.5
    r = lax.broadcasted_iota(jnp.int32, (tq, 2 * tq), 0)
    c = lax.broadcasted_iota(jnp.int32, (tq, 2 * tq), 1)
    dist = r + tq - c
    mask = (dist >= 0) & (dist < SWA_WINDOW) & ((c >= tq) | (nblk > 0))
    outs = []
    for kv in range(SWA_KV_HEADS):
        lo, hi = kv * hd, (kv + 1) * hd
        kb = jnp.concatenate([kp_ref[:, lo:hi], kc_ref[:, lo:hi]], axis=0)
        vb = jnp.concatenate([vp_ref[:, lo:hi], vc_ref[:, lo:hi]], axis=0)
        for g in range(grp):
            hq = kv * grp + g
            qh = q_ref[:, hq * hd:(hq + 1) * hd]
            s = _dot_nt(qh, kb) * scale + bias_ref[hq]
            s = jnp.where(mask, s, NEG_INF)
            sink = sink_ref[hq]
            m = jnp.maximum(jnp.max(s, axis=-1, keepdims=True), sink)
            e = jnp.exp(s - m)
            p = e / (jnp.sum(e, axis=-1, keepdims=True) + jnp.exp(sink - m))
            outs.append(_dot(p.astype(BF16), vb))
    o_ref[...] = jnp.concatenate(outs, axis=1).astype(o_ref.dtype)


def swa_attention(qkv, sinks, rel_bias, batch, seq):
    n = qkv.shape[0]
    tq = SWA_QBLOCK
    nb = seq // tq
    qd = SWA_Q_HEADS * SWA_HEAD_DIM
    kcol = qd // LANES
    dist = np.arange(tq)[:, None] + tq - np.arange(2 * tq)[None, :]
    bias = rel_bias.astype(F32)[_t5_bucket(dist)].transpose(2, 0, 1)

    def cur(col):
        return pl.BlockSpec((tq, LANES), lambda b, i: (b * nb + i, col))

    def prev(col):
        return pl.BlockSpec((tq, LANES), lambda b, i: (b * nb + jnp.maximum(i - 1, 0), col))

    return pl.pallas_call(
        _swa_kernel,
        grid=(batch, nb),
        in_specs=[pl.BlockSpec(memory_space=pltpu.SMEM),
                  pl.BlockSpec((tq, qd), lambda b, i: (b * nb + i, 0)),
                  cur(kcol), prev(kcol), cur(kcol + 1), prev(kcol + 1),
                  pl.BlockSpec((SWA_Q_HEADS, tq, 2 * tq), lambda b, i: (0, 0, 0))],
        out_specs=pl.BlockSpec((tq, qd), lambda b, i: (b * nb + i, 0)),
        out_shape=jax.ShapeDtypeStruct((n, qd), BF16),
        compiler_params=_cparams(("parallel", "arbitrary"), 40),
        name="swa_attention",
    )(sinks.astype(F32), qkv, qkv, qkv, qkv, qkv, bias)


def _mla_qkv_kernel(c_ref, pos_ref, invf_ref, qn_ref, kvn_ref, wq_ref, wkv_ref, q_ref, k_ref, v_ref):
    nh, dn = MLA_HEADS, MLA_NOPE
    cq = c_ref[:, :MLA_Q_RANK].astype(F32)
    ckv = c_ref[:, MLA_Q_RANK:MLA_Q_RANK + MLA_KV_RANK].astype(F32)
    base = MLA_Q_RANK + MLA_KV_RANK
    kr_a = c_ref[:, base:base + LANES].astype(F32)
    kr_b = c_ref[:, base + LANES:base + 2 * LANES].astype(F32)
    ang = pos_ref[...].astype(F32) * invf_ref[...]
    cos, sin = jnp.cos(ang), jnp.sin(ang)
    scale = (MLA_NOPE + MLA_ROPE) ** -0.5 * math.log2(math.e)
    q = _dot(_rms(cq, qn_ref[...]).astype(BF16), wq_ref[...]) * scale
    kv = _dot(_rms(ckv, kvn_ref[...]).astype(BF16), wkv_ref[...])
    k_rot = (kr_a * cos + kr_b * sin).astype(BF16)
    for h in range(nh):
        q_nope = q[:, h * dn:(h + 1) * dn]
        q_a = q[:, (nh + h) * dn:(nh + h + 1) * dn]
        q_b = q[:, (2 * nh + h) * dn:(2 * nh + h + 1) * dn]
        q_ref[:, 2 * h * dn:(2 * h + 1) * dn] = q_nope.astype(BF16)
        q_ref[:, (2 * h + 1) * dn:(2 * h + 2) * dn] = (q_a * cos + q_b * sin).astype(BF16)
        k_ref[:, 2 * h * dn:(2 * h + 1) * dn] = kv[:, h * dn:(h + 1) * dn].astype(BF16)
        k_ref[:, (2 * h + 1) * dn:(2 * h + 2) * dn] = k_rot
    v_ref[...] = kv[:, nh * dn:].astype(BF16)


def _rotate_half_cols(w):
    half = w.shape[-1] // 2
    return jnp.concatenate([-w[..., half:], w[..., :half]], axis=-1)


def mla_qkv(c, positions, q_norm, w_q_b, kv_norm, w_kv_b, *, tm=512):
    n = c.shape[0]
    tm = min(tm, n)
    nh, dn, dr, dv = MLA_HEADS, MLA_NOPE, MLA_ROPE, MLA_V
    wq = w_q_b.reshape(MLA_Q_RANK, nh, dn + dr)
    pad = jnp.zeros((MLA_Q_RANK, nh, LANES - dr), w_q_b.dtype)
    wq_rope = wq[:, :, dn:]
    wq_all = jnp.concatenate([
        wq[:, :, :dn].reshape(MLA_Q_RANK, nh * dn),
        jnp.concatenate([wq_rope, pad], axis=-1).reshape(MLA_Q_RANK, nh * LANES),
        jnp.concatenate([_rotate_half_cols(wq_rope), pad], axis=-1).reshape(MLA_Q_RANK, nh * LANES),
    ], axis=1).astype(BF16)
    wkv = w_kv_b.reshape(MLA_KV_RANK, nh, dn + dv)
    wkv_all = jnp.concatenate([wkv[:, :, :dn].reshape(MLA_KV_RANK, nh * dn),
                               wkv[:, :, dn:].reshape(MLA_KV_RANK, nh * dv)], axis=1).astype(BF16)
    half = dr // 2
    inv_freq = ROPE_THETA ** (-jnp.arange(half, dtype=F32) / half)
    invf = jnp.concatenate([inv_freq, inv_freq, jnp.zeros((LANES - dr,), F32)]).reshape(1, LANES)
    cw = c.shape[1]
    return pl.pallas_call(
        _mla_qkv_kernel,
        grid=(n // tm,),
        in_specs=[pl.BlockSpec((tm, cw), lambda i: (i, 0)),
                  pl.BlockSpec((tm, 1), lambda i: (i, 0)),
                  pl.BlockSpec((1, LANES), lambda i: (0, 0)),
                  pl.BlockSpec((1, MLA_Q_RANK), lambda i: (0, 0)),
                  pl.BlockSpec((1, MLA_KV_RANK), lambda i: (0, 0)),
                  pl.BlockSpec(wq_all.shape, lambda i: (0, 0)),
                  pl.BlockSpec(wkv_all.shape, lambda i: (0, 0))],
        out_specs=[pl.BlockSpec((tm, 2 * nh * dn), lambda i: (i, 0)),
                   pl.BlockSpec((tm, 2 * nh * dn), lambda i: (i, 0)),
                   pl.BlockSpec((tm, nh * dv), lambda i: (i, 0))],
        out_shape=[jax.ShapeDtypeStruct((n, 2 * nh * dn), BF16),
                   jax.ShapeDtypeStruct((n, 2 * nh * dn), BF16),
                   jax.ShapeDtypeStruct((n, nh * dv), BF16)],
        compiler_params=_cparams(("parallel",), 48),
        name="mla_qkv",
    )(c, positions.reshape(n, 1), invf, q_norm.reshape(1, -1).astype(F32), kv_norm.reshape(1, -1).astype(F32),
      wq_all, wkv_all)


def _mla_attn_kernel(q_ref, k_ref, v_ref, o_ref, *, tk):
    i = pl.program_id(2)
    tq = q_ref.shape[0]
    dq = q_ref.shape[1] // MLA_HP
    dv = MLA_V
    row = i * tq + lax.broadcasted_iota(jnp.int32, (tq, tk), 0)
    col = lax.broadcasted_iota(jnp.int32, (tq, tk), 1)
    g = MLA_HP

    def heads(x, w):
        return jnp.stack([x[:, hh * w:(hh + 1) * w] for hh in range(g)])

    q = heads(q_ref[...], dq)

    def block(kb, state, masked):
        m, l, acc = state
        start = pl.multiple_of(kb * tk, tk)
        s = _bmm_nt(q, heads(k_ref[pl.ds(start, tk), :], dq))
        if masked:
            s = jnp.where(col + kb * tk <= row, s, NEG_INF)
        m_new = jnp.maximum(m, jnp.max(s, axis=-1, keepdims=True))
        alpha = jnp.exp2(m - m_new)
        p = jnp.exp2(s - m_new)
        l = alpha * l + jnp.sum(p, axis=-1, keepdims=True)
        acc = alpha * acc + _bmm(p.astype(BF16), heads(v_ref[pl.ds(start, tk), :], dv))
        return m_new, l, acc

    last = (i * tq + tq - 1) // tk
    state = (jnp.full((g, tq, 1), NEG_INF, F32), jnp.zeros((g, tq, 1), F32), jnp.zeros((g, tq, dv), F32))
    state = lax.fori_loop(0, last, lambda kb, st: block(kb, st, False), state)
    _, l, acc = block(last, state, True)
    out = acc / l
    o_ref[...] = jnp.concatenate([out[hh] for hh in range(g)], axis=1).astype(o_ref.dtype)


def mla_attention(qcat, kcat, v, batch, seq):
    n = qcat.shape[0]
    tq, tk = min(MLA_TQ, seq), min(MLA_TK, seq)
    assert tk % tq == 0
    nq = seq // tq
    dq = MLA_HP * qcat.shape[1] // MLA_HEADS
    dv = MLA_HP * MLA_V
    return pl.pallas_call(
        functools.partial(_mla_attn_kernel, tk=tk),
        grid=(batch, MLA_HEADS // MLA_HP, nq),
        in_specs=[pl.BlockSpec((tq, dq), lambda b, h, i: (b * nq + i, h)),
                  pl.BlockSpec((seq, dq), lambda b, h, i: (b, h)),
                  pl.BlockSpec((seq, dv), lambda b, h, i: (b, h))],
        out_specs=pl.BlockSpec((tq, dv), lambda b, h, i: (b * nq + i, h)),
        out_shape=jax.ShapeDtypeStruct((n, MLA_HEADS * MLA_V), BF16),
        compiler_params=_cparams(("parallel", "parallel", "arbitrary"), 40),
        name="mla_attention",
    )(qcat, kcat, v)


def _router_kernel(x_ref, nw_ref, wr_ref, xn_ref, meta_ref, cnt_ref):
    i = pl.program_id(0)
    tm = x_ref.shape[0]

    @pl.when(i == 0)
    def _():
        cnt_ref[...] = jnp.zeros_like(cnt_ref)

    xn = _rms(x_ref[...], nw_ref[...])
    xn_ref[...] = xn
    logits = _dot_hi(xn, wr_ref[...])
    lane = lax.broadcasted_iota(jnp.int32, logits.shape, 1).astype(F32)
    logits = jnp.where(lane < N_EXPERTS, logits, NEG_INF)
    m1 = jnp.max(logits, axis=-1, keepdims=True)
    e1 = jnp.min(jnp.where(logits == m1, lane, float(LANES)), axis=-1, keepdims=True)
    rest = jnp.where(lane == e1, NEG_INF, logits)
    m2 = jnp.max(rest, axis=-1, keepdims=True)
    e2 = jnp.min(jnp.where(rest == m2, lane, float(LANES)), axis=-1, keepdims=True)
    ex = jnp.exp(m2 - m1)
    w1 = 1.0 / (1.0 + ex)
    w2 = ex / (1.0 + ex)
    oh1 = jnp.where(lane == e1, 1.0, 0.0)
    oh2 = jnp.where(lane == e2, 1.0, 0.0)
    oh = (oh1 + oh2).astype(BF16)
    r = lax.broadcasted_iota(jnp.int32, (tm, tm), 0)
    c = lax.broadcasted_iota(jnp.int32, (tm, tm), 1)
    before = _dot(jnp.where(c < r, 1.0, 0.0).astype(BF16), oh)
    base = before + cnt_ref[0:1, :]
    rank1 = jnp.sum(oh1 * base, axis=-1, keepdims=True)
    rank2 = jnp.sum(oh2 * base, axis=-1, keepdims=True)
    cnt_ref[...] = cnt_ref[...] + jnp.sum(oh.astype(F32), axis=0, keepdims=True)
    ml = lax.broadcasted_iota(jnp.int32, meta_ref.shape, 1)
    meta = jnp.where(ml == 0, e1, 0.0)
    meta = jnp.where(ml == 1, e2, meta)
    meta = jnp.where(ml == 2, rank1, meta)
    meta = jnp.where(ml == 3, rank2, meta)
    meta = jnp.where(ml == 4, w1, meta)
    meta = jnp.where(ml == 5, w2, meta)
    meta_ref[...] = meta


def moe_router(h, nw, w_router, *, tm=512):
    m, d = h.shape
    tm = min(tm, m)
    wr = jnp.zeros((d, LANES), F32).at[:, :N_EXPERTS].set(w_router.astype(F32))
    return pl.pallas_call(
        _router_kernel,
        grid=(m // tm,),
        in_specs=[pl.BlockSpec((tm, d), lambda i: (i, 0)),
                  pl.BlockSpec((1, d), lambda i: (0, 0)),
                  pl.BlockSpec((d, LANES), lambda i: (0, 0))],
        out_specs=[pl.BlockSpec((tm, d), lambda i: (i, 0)),
                   pl.BlockSpec((tm, 8), lambda i: (i, 0)),
                   pl.BlockSpec((8, LANES), lambda i: (0, 0))],
        out_shape=[jax.ShapeDtypeStruct((m, d), F32),
                   jax.ShapeDtypeStruct((m, 8), F32),
                   jax.ShapeDtypeStruct((8, LANES), F32)],
        compiler_params=_cparams(("arbitrary",), 40),
        name="moe_router",
    )(h, nw.reshape(1, d), wr)


GATHER_UNROLL = 8


def _row_copy(src_hbm, src_row, dst_ref, r, sem):
    return pltpu.make_async_copy(src_hbm.at[pl.ds(src_row, 1)], dst_ref.at[pl.ds(r, 1)], sem)


def _gather_start(idx_hbm, chunk, src_hbm, dst_ref, idx_smem, sem_idx, sem_rows):
    n = dst_ref.shape[0]
    off = pl.multiple_of(chunk * n, n)
    cp = pltpu.make_async_copy(idx_hbm.at[pl.ds(off, n)], idx_smem, sem_idx)
    cp.start()
    cp.wait()

    def issue(r, carry):
        _row_copy(src_hbm, idx_smem[r], dst_ref, r, sem_rows).start()
        return carry

    lax.fori_loop(0, n, issue, 0, unroll=GATHER_UNROLL)


def _gather_wait(src_hbm, dst_ref, sem_rows):
    def drain(r, carry):
        _row_copy(src_hbm, 0, dst_ref, r, sem_rows).wait()
        return carry

    lax.fori_loop(0, dst_ref.shape[0], drain, 0, unroll=GATHER_UNROLL)


def _moe_kernel(te_ref, nt_ref, idx_hbm, x_hbm, wg_ref, wu_ref, wd_ref, sw_ref, o_ref,
                xf_ref, xb_ref, acc_ref, idx_smem, sem_idx, sem_rows):
    i = pl.program_id(0)
    j = pl.program_id(1)
    n_used = nt_ref[0]

    @pl.when(i < n_used)
    def _():
        @pl.when(j == 0)
        def _():
            slot = i % 2
            @pl.when(i == 0)
            def _():
                _gather_start(idx_hbm, i, x_hbm, xf_ref.at[slot], idx_smem, sem_idx, sem_rows.at[slot])

            _gather_wait(x_hbm, xf_ref.at[slot], sem_rows.at[slot])

            @pl.when(i + 1 < n_used)
            def _():
                _gather_start(idx_hbm, i + 1, x_hbm, xf_ref.at[1 - slot], idx_smem, sem_idx, sem_rows.at[1 - slot])

            xb_ref[...] = xf_ref[slot].astype(BF16)
            acc_ref[...] = jnp.zeros_like(acc_ref)

        x = xb_ref[...]
        a = (_silu(_dot(x, wg_ref[0])) * _dot(x, wu_ref[0])).astype(BF16)
        acc_ref[...] += _dot(a, wd_ref[0])

    @pl.when(j == pl.num_programs(1) - 1)
    def _():
        o_ref[...] = acc_ref[...] * sw_ref[...]


def moe_experts(xn, sorted_tok, sorted_w, tile_expert, n_tiles_used, w_gate_up, w_down, *, tf=512):
    n, d = xn.shape
    ne, f, _ = w_down.shape
    tm = MOE_TILE
    p = sorted_tok.shape[0]
    n_tiles = p // tm
    tf = min(tf, f)
    nf = f // tf
    grid_spec = pltpu.PrefetchScalarGridSpec(
        num_scalar_prefetch=2,
        grid=(n_tiles, nf),
        in_specs=[pl.BlockSpec(memory_space=pl.ANY),
                  pl.BlockSpec(memory_space=pl.ANY),
                  pl.BlockSpec((1, d, tf), lambda i, j, te, nt: (te[i], 0, j)),
                  pl.BlockSpec((1, d, tf), lambda i, j, te, nt: (te[i], 0, j + nf)),
                  pl.BlockSpec((1, tf, d), lambda i, j, te, nt: (te[i], j, 0)),
                  pl.BlockSpec((tm, 1), lambda i, j, te, nt: (i, 0))],
        out_specs=pl.BlockSpec((tm, d), lambda i, j, te, nt: (i, 0)),
        scratch_shapes=[pltpu.VMEM((2, tm, d), F32), pltpu.VMEM((tm, d), BF16), pltpu.VMEM((tm, d), F32),
                        pltpu.SMEM((tm,), jnp.int32), pltpu.SemaphoreType.DMA, pltpu.SemaphoreType.DMA((2,))],
    )
    return pl.pallas_call(
        _moe_kernel,
        grid_spec=grid_spec,
        out_shape=jax.ShapeDtypeStruct((p, d), F32),
        compiler_params=_cparams(("arbitrary", "arbitrary"), 56),
        name="moe_experts",
    )(tile_expert, n_tiles_used, sorted_tok, xn, w_gate_up, w_gate_up, w_down, sorted_w.reshape(p, 1))


def _combine_kernel(idx_hbm, y_hbm, h_ref, nw_ref, o_ref, buf_ref, idx_smem, sem_idx, sem_rows, *, final_norm):
    i = pl.program_id(0)
    tc = h_ref.shape[0]
    slot = i % 2

    @pl.when(i == 0)
    def _():
        _gather_start(idx_hbm, i, y_hbm, buf_ref.at[slot], idx_smem, sem_idx, sem_rows.at[slot])

    _gather_wait(y_hbm, buf_ref.at[slot], sem_rows.at[slot])

    @pl.when(i + 1 < pl.num_programs(0))
    def _():
        _gather_start(idx_hbm, i + 1, y_hbm, buf_ref.at[1 - slot], idx_smem, sem_idx, sem_rows.at[1 - slot])

    out = h_ref[...] + buf_ref[slot, :tc, :] + buf_ref[slot, tc:, :]
    if final_norm:
        out = _rms(out, nw_ref[...])
    o_ref[...] = out


def moe_combine(h, y_sorted, pos_tiles, nw, *, final_norm):
    n, d = h.shape
    tc = MOE_TILE // TOP_K
    return pl.pallas_call(
        functools.partial(_combine_kernel, final_norm=final_norm),
        grid=(n // tc,),
        in_specs=[pl.BlockSpec(memory_space=pl.ANY),
                  pl.BlockSpec(memory_space=pl.ANY),
                  pl.BlockSpec((tc, d), lambda i: (i, 0)),
                  pl.BlockSpec((1, d), lambda i: (0, 0))],
        out_specs=pl.BlockSpec((tc, d), lambda i: (i, 0)),
        out_shape=jax.ShapeDtypeStruct((n, d), F32),
        scratch_shapes=[pltpu.VMEM((2, TOP_K * tc, d), F32), pltpu.SMEM((TOP_K * tc,), jnp.int32),
                        pltpu.SemaphoreType.DMA, pltpu.SemaphoreType.DMA((2,))],
        compiler_params=_cparams(("arbitrary",), 40),
        name="moe_combine",
    )(pos_tiles, y_sorted, h, nw.reshape(1, d))


def routed_swiglu(h, nw, w_router, w_gate_up, w_down, final_nw):
    n, d = h.shape
    tm = MOE_TILE
    tc = tm // TOP_K
    xn, meta, cnt = moe_router(h, nw, w_router)
    counts = cnt[0, :N_EXPERTS].astype(jnp.int32)
    tiles_per = (counts + tm - 1) // tm
    tile_end = jnp.cumsum(tiles_per)
    start = (tile_end - tiles_per) * tm
    e = meta[:, 0:2].astype(jnp.int32)
    pos = start[e] + meta[:, 2:4].astype(jnp.int32)
    p = TOP_K * n + N_EXPERTS * tm
    n_tiles = p // tm
    assign = jnp.full((p,), -1, jnp.int32).at[pos.reshape(-1)].set(
        jnp.arange(TOP_K * n, dtype=jnp.int32), unique_indices=True)
    valid = assign >= 0
    sorted_tok = jnp.where(valid, assign // TOP_K, 0)
    sorted_w = jnp.where(valid, meta[:, 4:6].reshape(-1)[jnp.maximum(assign, 0)], 0.0)
    tile_expert = jnp.minimum(jnp.searchsorted(tile_end, jnp.arange(n_tiles, dtype=jnp.int32), side="right"),
                              N_EXPERTS - 1).astype(jnp.int32)
    y = moe_experts(xn, sorted_tok, sorted_w, tile_expert, tile_end[-1:].astype(jnp.int32), w_gate_up, w_down)
    pos_tiles = pos.reshape(n // tc, tc, TOP_K).transpose(0, 2, 1).reshape(-1)
    return moe_combine(h, y, pos_tiles, final_nw if final_nw is not None else nw, final_norm=final_nw is not None)


def kernel(x, positions, rel_bias, attn_norm, ffn_norm, final_norm, gdn_w_in, gdn_conv_w, gdn_a_log, gdn_dt_bias,
           gdn_norm_w, gdn_w_out, sb_w_in, sb_w_out, swa_w_in, swa_sinks, swa_w_out, mla_w_in, mla_q_norm,
           mla_w_q_b, mla_kv_norm, mla_w_kv_b, mla_w_out, ffn_w_gate_up, ffn_w_down, moe_w_router, moe_w_gate_up,
           moe_w_down):
    batch, seq, d = x.shape
    n = batch * seq
    depth = attn_norm.shape[0]
    h = x.reshape(n, d).astype(F32)
    bf = lambda a: a.astype(BF16)
    for i in range(depth):
        mixer, occ = i % 4, i // 4
        if mixer == 0:
            w_in = gdn_w_in[occ]
            qkvz_dim = 4 * GDN_HEADS * GDN_HEAD_DIM
            qkvz = norm_matmul(h, attn_norm[i], bf(w_in[:, :qkvz_dim]))
            gcol, grow = gdn_gates(h, attn_norm[i], w_in[:, qkvz_dim:].astype(F32), gdn_a_log[occ], gdn_dt_bias[occ])
            o = gdn_core(qkvz, gdn_conv_w[occ].astype(F32), gcol, grow, gdn_norm_w[occ].astype(F32), batch, seq)
            h = matmul_residual(o, bf(gdn_w_out[occ]), h)
        elif mixer == 1:
            qkv = norm_matmul(h, attn_norm[i], bf(sb_w_in[occ]))
            o = sb_attention(qkv, batch, seq)
            h = matmul_residual(o, bf(sb_w_out[occ]), h)
        elif mixer == 2:
            qkv = norm_matmul(h, attn_norm[i], bf(swa_w_in[occ]), tn=256)
            o = swa_attention(qkv, swa_sinks[occ], rel_bias, batch, seq)
            h = matmul_residual(o, bf(swa_w_out[occ]), h)
        else:
            w_in = mla_w_in[occ]
            base = MLA_Q_RANK + MLA_KV_RANK
            kr = w_in[:, base:]
            zpad = jnp.zeros((d, LANES - MLA_ROPE), w_in.dtype)
            w_all = jnp.concatenate([w_in[:, :base], kr, zpad, _rotate_half_cols(kr), zpad], axis=1)
            c = norm_matmul(h, attn_norm[i], bf(w_all), tn=w_all.shape[1], out_dtype=F32)
            qcat, kcat, v = mla_qkv(c, positions, mla_q_norm[occ], mla_w_q_b[occ], mla_kv_norm[occ], mla_w_kv_b[occ])
            o = mla_attention(qcat, kcat, v, batch, seq)
            h = matmul_residual(o, bf(mla_w_out[occ]), h)
        f = i // 2
        last = i == depth - 1
        if i % 2 == 0:
            h = ffn_dense(h, ffn_norm[i], bf(ffn_w_gate_up[f]), bf(ffn_w_down[f]))
            if last:
                h = final_rmsnorm(h, final_norm)
        else:
            h = routed_swiglu(h, ffn_norm[i], moe_w_router[f], bf(moe_w_gate_up[f]), bf(moe_w_down[f]),
                              final_norm if last else None)
    return h.reshape(batch, seq, d).astype(x.dtype)


def _final_norm_kernel(x_ref, nw_ref, o_ref):
    o_ref[...] = _rms(x_ref[...], nw_ref[...])


def final_rmsnorm(h, nw, *, tm=512):
    m, d = h.shape
    tm = min(tm, m)
    return pl.pallas_call(
        _final_norm_kernel,
        grid=(m // tm,),
        in_specs=[pl.BlockSpec((tm, d), lambda i: (i, 0)), pl.BlockSpec((1, d), lambda i: (0, 0))],
        out_specs=pl.BlockSpec((tm, d), lambda i: (i, 0)),
        out_shape=jax.ShapeDtypeStruct((m, d), F32),
        compiler_params=_cparams(("parallel",), 40),
        name="final_rmsnorm",
    )(h, nw.reshape(1, d))
```

```python
import functools
import math

import numpy as np
import jax
import jax.numpy as jnp
from jax import lax
from jax.experimental import pallas as pl
from jax.experimental.pallas import tpu as pltpu

F32 = jnp.float32
BF16 = jnp.bfloat16
HIGHEST = lax.Precision.HIGHEST

NORM_EPS = 1e-6
NEG_INF = -1e30
LANES = 128

GDN_HEADS, GDN_HEAD_DIM, GDN_CONV = 8, 128, 4
GDN_BLOCK = 256
GDN_INV_BASE = 16
GDN_HP = 4
SB_HEADS, SB_HEAD_DIM, SB_BLOCK = 16, 64, 256
SB_HP = 8
SB_ZERO_EXP = 160.0
SWA_Q_HEADS, SWA_KV_HEADS, SWA_HEAD_DIM, SWA_WINDOW, SWA_QBLOCK = 16, 2, 64, 128, 128
REL_BUCKETS, REL_MAX_DIST = 32, 128
MLA_HEADS, MLA_Q_RANK, MLA_KV_RANK, MLA_NOPE, MLA_ROPE, MLA_V = 8, 384, 256, 128, 64, 128
MLA_HP = 2
MLA_TQ, MLA_TK = 512, 1024
ROPE_THETA = 10000.0
N_EXPERTS, TOP_K = 8, 2
MOE_TILE = 1024


def _cparams(sem, vmem_mb):
    return pltpu.CompilerParams(dimension_semantics=sem, vmem_limit_bytes=vmem_mb << 20)


def _dot(a, b):
    return jnp.dot(a, b, preferred_element_type=F32)


def _dot_nt(a, b):
    return lax.dot_general(a, b, (((1,), (1,)), ((), ())), preferred_element_type=F32)


def _bmm(a, b):
    return jnp.einsum("gmk,gkn->gmn", a, b, preferred_element_type=F32)


def _bmm_nt(a, b):
    return jnp.einsum("gmk,gnk->gmn", a, b, preferred_element_type=F32)


def _dot_hi(a, b):
    return jnp.dot(a, b, preferred_element_type=F32, precision=HIGHEST)


def _dot_nt_hi(a, b):
    return lax.dot_general(a, b, (((1,), (1,)), ((), ())), preferred_element_type=F32, precision=HIGHEST)


def _rms(x, w):
    return x * lax.rsqrt(jnp.mean(x * x, axis=-1, keepdims=True) + NORM_EPS) * w


def _silu(x):
    return x * jax.nn.sigmoid(x)


def _softplus(x):
    return jnp.maximum(x, 0.0) + jnp.log(1.0 + jnp.exp(-jnp.abs(x)))


def _norm_matmul_kernel(x_ref, nw_ref, w_ref, o_ref, xn_ref):
    @pl.when(pl.program_id(1) == 0)
    def _():
        xn_ref[...] = _rms(x_ref[...], nw_ref[...]).astype(BF16)

    o_ref[...] = _dot(xn_ref[...], w_ref[...]).astype(o_ref.dtype)


def norm_matmul(x, nw, w, *, tm=1024, tn=512, out_dtype=BF16):
    m, d = x.shape
    n = w.shape[1]
    tm, tn = min(tm, m), min(tn, n)
    return pl.pallas_call(
        _norm_matmul_kernel,
        grid=(m // tm, n // tn),
        in_specs=[pl.BlockSpec((tm, d), lambda i, j: (i, 0)),
                  pl.BlockSpec((1, d), lambda i, j: (0, 0)),
                  pl.BlockSpec((d, tn), lambda i, j: (0, j))],
        out_specs=pl.BlockSpec((tm, tn), lambda i, j: (i, j)),
        out_shape=jax.ShapeDtypeStruct((m, n), out_dtype),
        scratch_shapes=[pltpu.VMEM((tm, d), BF16)],
        compiler_params=_cparams(("parallel", "arbitrary"), 40),
        name="norm_matmul",
    )(x, nw.reshape(1, d), w)


def _matmul_residual_kernel(a_ref, w_ref, h_ref, o_ref):
    o_ref[...] = h_ref[...] + _dot(a_ref[...], w_ref[...])


def matmul_residual(a, w, h, *, tm=512):
    m, k = a.shape
    d = w.shape[1]
    tm = min(tm, m)
    return pl.pallas_call(
        _matmul_residual_kernel,
        grid=(m // tm,),
        in_specs=[pl.BlockSpec((tm, k), lambda i: (i, 0)),
                  pl.BlockSpec((k, d), lambda i: (0, 0)),
                  pl.BlockSpec((tm, d), lambda i: (i, 0))],
        out_specs=pl.BlockSpec((tm, d), lambda i: (i, 0)),
        out_shape=jax.ShapeDtypeStruct((m, d), F32),
        compiler_params=_cparams(("parallel",), 40),
        name="matmul_residual",
    )(a, w, h)


def _ffn_kernel(x_ref, nw_ref, wg_ref, wu_ref, wd_ref, o_ref, xn_ref, acc_ref):
    j = pl.program_id(1)

    @pl.when(j == 0)
    def _():
        x = x_ref[...]
        xn_ref[...] = _rms(x, nw_ref[...]).astype(BF16)
        acc_ref[...] = x

    xn = xn_ref[...]
    a = (_silu(_dot(xn, wg_ref[...])) * _dot(xn, wu_ref[...])).astype(BF16)
    acc_ref[...] += _dot(a, wd_ref[...])

    @pl.when(j == pl.num_programs(1) - 1)
    def _():
        o_ref[...] = acc_ref[...]


def ffn_dense(h, nw, w_gate_up, w_down, *, tm=1024, tf=256):
    m, d = h.shape
    f = w_down.shape[0]
    tm, tf = min(tm, m), min(tf, f)
    nf = f // tf
    return pl.pallas_call(
        _ffn_kernel,
        grid=(m // tm, nf),
        in_specs=[pl.BlockSpec((tm, d), lambda i, j: (i, 0)),
                  pl.BlockSpec((1, d), lambda i, j: (0, 0)),
                  pl.BlockSpec((d, tf), lambda i, j: (0, j)),
                  pl.BlockSpec((d, tf), lambda i, j: (0, j + nf)),
                  pl.BlockSpec((tf, d), lambda i, j: (j, 0))],
        out_specs=pl.BlockSpec((tm, d), lambda i, j: (i, 0)),
        out_shape=jax.ShapeDtypeStruct((m, d), F32),
        scratch_shapes=[pltpu.VMEM((tm, d), BF16), pltpu.VMEM((tm, d), F32)],
        compiler_params=_cparams(("parallel", "arbitrary"), 48),
        name="ffn_dense",
    )(h, nw.reshape(1, d), w_gate_up, w_gate_up, w_down)


def _gdn_gates_kernel(x_ref, nw_ref, w_ref, pcol_ref, col_ref, row_ref):
    tm = x_ref.shape[0]
    nh = GDN_HEADS
    xn = _rms(x_ref[...], nw_ref[...])
    lc = _dot_hi(xn, w_ref[...])
    ci = lax.broadcasted_iota(jnp.int32, lc.shape, 1)
    is_g_c = (ci >= nh) & (ci < 2 * nh)
    g_c = jnp.where(is_g_c, pcol_ref[0:1, :] * _softplus(lc + pcol_ref[1:2, :]), 0.0)
    r = lax.broadcasted_iota(jnp.int32, (tm, tm), 0)
    c = lax.broadcasted_iota(jnp.int32, (tm, tm), 1)
    lower = jnp.where(((r // GDN_BLOCK) == (c // GDN_BLOCK)) & (c <= r), 1.0, 0.0).astype(F32)
    col = jnp.where(ci < nh, jax.nn.sigmoid(lc), _dot_hi(lower, g_c))
    col_ref[...] = col
    row_ref[...] = col.T[:2 * nh]


def gdn_gates(h, nw, w_bd, a_log, dt_bias, *, tm=512):
    m, d = h.shape
    tm = min(tm, m)
    nh = GDN_HEADS
    w_pad = jnp.zeros((d, LANES), F32).at[:, :2 * nh].set(w_bd)
    z = jnp.zeros((nh,), F32)
    prm = jnp.stack([jnp.concatenate([z, -jnp.exp(a_log.astype(F32))]),
                     jnp.concatenate([z, dt_bias.astype(F32)])])
    pcol = jnp.zeros((8, LANES), F32).at[:2, :2 * nh].set(prm)
    return pl.pallas_call(
        _gdn_gates_kernel,
        grid=(m // tm,),
        in_specs=[pl.BlockSpec((tm, d), lambda i: (i, 0)),
                  pl.BlockSpec((1, d), lambda i: (0, 0)),
                  pl.BlockSpec((d, LANES), lambda i: (0, 0)),
                  pl.BlockSpec((8, LANES), lambda i: (0, 0))],
        out_specs=[pl.BlockSpec((tm, LANES), lambda i: (i, 0)),
                   pl.BlockSpec((2 * nh, tm), lambda i: (0, i))],
        out_shape=[jax.ShapeDtypeStruct((m, LANES), F32), jax.ShapeDtypeStruct((2 * nh, m), F32)],
        compiler_params=_cparams(("parallel",), 40),
        name="gdn_gates",
    )(h, nw.reshape(1, d), w_pad, pcol)


def _unit_lower_inverse(low, eye, rc_xor):
    n = low.shape[-1]
    b = GDN_INV_BASE
    nb = jnp.where(rc_xor < b, -low, 0.0)
    nb_b = nb.astype(BF16)
    n2 = _bmm(nb_b, nb_b)
    n2_b = n2.astype(BF16)
    n4 = _bmm(n2_b, n2_b)
    n4_b = n4.astype(BF16)
    n8 = _bmm(n4_b, n4_b)
    p_off = nb + n2 + _bmm(nb_b, n2_b)
    q = eye + n4 + n8 + _bmm(n4_b, n8.astype(BF16))
    inv = q + _bmm(p_off.astype(BF16), q.astype(BF16))
    while b < n:
        m = jnp.where((rc_xor >> int(math.log2(b))) == 1, low, 0.0).astype(BF16)
        inv_b = inv.astype(BF16)
        inv = inv - _bmm(inv_b, _bmm(m, inv_b).astype(BF16))
        b *= 2
    return inv


def _gdn_core_kernel(q_ref, k_ref, v_ref, z_ref, cwq_ref, cwk_ref, cwv_ref, col_ref, row_ref, nw_ref,
                     o_ref, s_ref, tail_ref, ext_ref):
    hp = pl.program_id(1)
    t = pl.program_id(2)
    tb = q_ref.shape[0]
    dk = GDN_HEAD_DIM
    nh = GDN_HEADS

    @pl.when(t == 0)
    def _():
        s_ref[...] = jnp.zeros_like(s_ref)
        tail_ref[...] = jnp.zeros_like(tail_ref)

    r = lax.broadcasted_iota(jnp.int32, (tb, tb), 0)
    c = lax.broadcasted_iota(jnp.int32, (tb, tb), 1)
    rc_xor = jnp.bitwise_xor(r, c)
    eye = jnp.where(r == c, 1.0, 0.0).astype(F32)
    incl = c <= r
    strict = c < r
    colv = col_ref[...]
    ci = lax.broadcasted_iota(jnp.int32, colv.shape, 1)

    def conv_silu(slot, x_ref, cw_ref, lanes):
        x = x_ref[:, lanes].astype(F32)
        ext_ref[slot, 0:8, :] = tail_ref[slot]
        ext_ref[slot, 8:, :] = x
        tail_ref[slot] = x[tb - 8:, :]
        acc = x * cw_ref[GDN_CONV - 1:GDN_CONV, lanes]
        for s in range(1, GDN_CONV):
            acc = acc + ext_ref[slot, 8 - s:8 - s + tb, :] * cw_ref[GDN_CONV - 1 - s:GDN_CONV - s, lanes]
        return _silu(acc)

    per_head = []
    for j in range(GDN_HP):
        h = hp * GDN_HP + j
        lanes = slice(j * dk, (j + 1) * dk)
        q = conv_silu(3 * j, q_ref, cwq_ref, lanes)
        k = conv_silu(3 * j + 1, k_ref, cwk_ref, lanes)
        v = conv_silu(3 * j + 2, v_ref, cwv_ref, lanes)
        qn = q * (lax.rsqrt(jnp.sum(q * q, axis=-1, keepdims=True) + NORM_EPS) * dk ** -0.5)
        kn = k * lax.rsqrt(jnp.sum(k * k, axis=-1, keepdims=True) + NORM_EPS)
        beta = jnp.sum(jnp.where(ci == h, colv, 0.0), axis=1, keepdims=True)
        gc = jnp.sum(jnp.where(ci == h + nh, colv, 0.0), axis=1, keepdims=True)
        gr = row_ref[pl.ds(h + nh, 1), :]
        per_head.append((qn, kn, v, beta, gc, gr))

    qn, kn, v, beta, gc, gr = (jnp.stack(z) for z in zip(*per_head))
    g_last = gr[:, :, tb - 1:tb]
    kn_b = kn.astype(BF16)
    decay = jnp.where(incl, jnp.exp(jnp.where(incl, gc - gr, 0.0)), 0.0)
    kb = kn * beta
    low = jnp.where(strict, _bmm_nt(kb.astype(BF16), kn_b) * decay, 0.0)
    tinv = _unit_lower_inverse(low, eye, rc_xor)
    rhs = jnp.concatenate([v * beta, kb * jnp.exp(gc)], axis=2)
    uw = _bmm(tinv.astype(BF16), rhs.astype(BF16))
    u, w = uw[:, :, :dk], uw[:, :, dk:]
    attn = (_bmm_nt(qn.astype(BF16), kn_b) * decay).astype(BF16)

    s = s_ref[...]
    sb = s.astype(BF16)
    v_new = u - _bmm(w.astype(BF16), sb)
    v_new_b = v_new.astype(BF16)
    o = _bmm((qn * jnp.exp(gc)).astype(BF16), sb) + _bmm(attn, v_new_b)
    k_dec = kn * jnp.exp(g_last - gc)
    k_dec_t = jnp.stack([k_dec[j].T for j in range(GDN_HP)]).astype(BF16)
    s_ref[...] = s * jnp.exp(g_last) + _bmm(k_dec_t, v_new_b)

    for j in range(GDN_HP):
        lanes = slice(j * dk, (j + 1) * dk)
        o_ref[:, lanes] = (_rms(o[j], nw_ref[...]) * _silu(z_ref[:, lanes].astype(F32))).astype(o_ref.dtype)


def gdn_core(qkvz, conv_w, gcol, grow, norm_w, batch, seq):
    n = qkvz.shape[0]
    nh, dk, tb, hp = GDN_HEADS, GDN_HEAD_DIM, min(GDN_BLOCK, seq), GDN_HP
    nt = seq // tb
    ng = nh // hp

    def tok(sec):
        return pl.BlockSpec((tb, hp * dk), lambda b, g, t: (b * nt + t, g + sec * ng))

    def cw(sec):
        return pl.BlockSpec((GDN_CONV, hp * dk), lambda b, g, t: (0, g + sec * ng))

    return pl.pallas_call(
        _gdn_core_kernel,
        grid=(batch, ng, nt),
        in_specs=[tok(0), tok(1), tok(2), tok(3), cw(0), cw(1), cw(2),
                  pl.BlockSpec((tb, LANES), lambda b, g, t: (b * nt + t, 0)),
                  pl.BlockSpec((2 * nh, tb), lambda b, g, t: (0, b * nt + t)),
                  pl.BlockSpec((1, dk), lambda b, g, t: (0, 0))],
        out_specs=pl.BlockSpec((tb, hp * dk), lambda b, g, t: (b * nt + t, g)),
        out_shape=jax.ShapeDtypeStruct((n, nh * dk), BF16),
        scratch_shapes=[pltpu.VMEM((hp, dk, dk), F32), pltpu.VMEM((3 * hp, 8, dk), F32),
                        pltpu.VMEM((3 * hp, tb + 8, dk), F32)],
        compiler_params=_cparams(("parallel", "parallel", "arbitrary"), 40),
        name="gdn_core",
    )(qkvz, qkvz, qkvz, qkvz, conv_w, conv_w, conv_w, gcol, grow, norm_w.reshape(1, dk))


def _sb_attn_kernel(q_ref, k_ref, v_ref, o_ref, kmax_ref):
    i = pl.program_id(2)
    tq = q_ref.shape[0]
    hd = SB_HEAD_DIM
    nh = SB_HP
    log2e = math.log2(math.e)
    r = lax.broadcasted_iota(jnp.int32, (tq, tq), 0)
    c = lax.broadcasted_iota(jnp.int32, (tq, tq), 1)
    upper = jnp.where(r >= c, 1.0, 0.0).astype(BF16)
    strict = c < r
    def heads(x):
        return jnp.stack([x[:, hh * hd:(hh + 1) * hd] for hh in range(nh)])

    qh = heads((q_ref[...].astype(F32) * (hd ** -0.5 * log2e)).astype(BF16))

    @pl.when(i == 0)
    def _():
        lane = lax.broadcasted_iota(jnp.int32, (nh * hd, LANES), 0)
        col = lax.broadcasted_iota(jnp.int32, (nh * hd, LANES), 1)
        ind = jnp.where(lane // hd == col, 1.0, 0.0).astype(BF16)

        def chunk_max(cb, m):
            kf = k_ref[pl.ds(pl.multiple_of(cb * tq, tq), tq), :].astype(F32)
            return jnp.maximum(m, _dot((kf * kf).astype(BF16), ind))

        m = lax.fori_loop(0, k_ref.shape[0] // tq, chunk_max, jnp.zeros((tq, LANES), F32))
        kmax_ref[...] = jnp.broadcast_to(jnp.max(jnp.max(m, axis=0, keepdims=True), axis=1, keepdims=True),
                                         kmax_ref.shape)

    qf = qh.astype(F32)
    q2max = jnp.max(jnp.max(jnp.sum(qf * qf, axis=-1, keepdims=True), axis=0), axis=0, keepdims=True)
    w_bound = jnp.sqrt(q2max * kmax_ref[0:1, 0:1]) * 1.02

    def block(kb, state, masked):
        carry, acc = state
        start = pl.multiple_of(kb * tq, tq)
        kblk = heads(k_ref[pl.ds(start, tq), :])
        vblk = heads(v_ref[pl.ds(start, tq), :])
        w = jnp.einsum("hqd,hkd->hqk", qh, kblk, preferred_element_type=F32)
        sp = jnp.maximum(w, 0.0) + jnp.log(1.0 + jnp.exp2(-jnp.abs(w))) * log2e
        if masked:
            sp = jnp.where(strict, sp, 0.0)
        rloc = _dot(sp.astype(BF16).reshape(nh * tq, tq), upper).reshape(nh, tq, tq)
        a = jnp.exp2(w - rloc - carry)
        if masked:
            a = jnp.where(strict, a, 0.0)
        acc = acc + jnp.einsum("hqk,hkd->hqd", a.astype(BF16), vblk, preferred_element_type=F32)
        return carry + rloc[:, :, 0:1], acc

    def slack(carry):
        return jnp.sum(jnp.min(jnp.min(carry, axis=0), axis=0, keepdims=True) - w_bound)

    state = (jnp.zeros((nh, tq, 1), F32), jnp.zeros((nh, tq, hd), F32))
    carry, acc = block(i, state, True)

    def cond(st):
        return (st[0] < i) & (st[3] < SB_ZERO_EXP)

    def body(st):
        n, carry, acc, _ = st
        carry, acc = block(i - 1 - n, (carry, acc), False)
        return n + 1, carry, acc, slack(carry)

    _, _, acc, _ = lax.while_loop(cond, body, (jnp.int32(0), carry, acc, slack(carry)))
    o_ref[...] = jnp.concatenate([acc[hh] for hh in range(nh)], axis=1).astype(o_ref.dtype)


def sb_attention(qkv, batch, seq):
    n = qkv.shape[0]
    tq = min(SB_BLOCK, seq)
    nq = seq // tq
    gw = SB_HP * SB_HEAD_DIM
    npair = SB_HEADS // SB_HP
    return pl.pallas_call(
        _sb_attn_kernel,
        grid=(batch, npair, nq),
        in_specs=[pl.BlockSpec((tq, gw), lambda b, p, i: (b * nq + i, p)),
                  pl.BlockSpec((seq, gw), lambda b, p, i: (b, npair + p)),
                  pl.BlockSpec((seq, gw), lambda b, p, i: (b, 2 * npair + p))],
        out_specs=pl.BlockSpec((tq, gw), lambda b, p, i: (b * nq + i, p)),
        out_shape=jax.ShapeDtypeStruct((n, SB_HEADS * SB_HEAD_DIM), BF16),
        scratch_shapes=[pltpu.VMEM((8, LANES), F32)],
        compiler_params=_cparams(("parallel", "parallel", "arbitrary"), 40),
        name="sb_attention",
    )(qkv, qkv, qkv)


def _t5_bucket(dist):
    exact = REL_BUCKETS // 2
    n = np.maximum(dist, 0)
    log_ratio = (np.log(np.maximum(n, 1).astype(np.float32) / exact)
                 / np.log(np.float32(REL_MAX_DIST / exact)))
    large = np.minimum(exact + (log_ratio * (REL_BUCKETS - exact)).astype(np.int32), REL_BUCKETS - 1)
    return np.where(n < exact, n, large).astype(np.int32)


def _swa_kernel(sink_ref, q_ref, kc_ref, kp_ref, vc_ref, vp_ref, bias_ref, o_ref):
    nblk = pl.program_id(1)
    tq = q_ref.shape[0]
    hd = SWA_HEAD_DIM
    grp = SWA_Q_HEADS // SWA_KV_HEADS
    scale = hd ** -0.5
    r = lax.broadcasted_iota(jnp.int32, (tq, 2 * tq), 0)
    c = lax.broadcasted_iota(jnp.int32, (tq, 2 * tq), 1)
    dist = r + tq - c
    mask = (dist >= 0) & (dist < SWA_WINDOW) & ((c >= tq) | (nblk > 0))
    outs = []
    for kv in range(SWA_KV_HEADS):
        lo, hi = kv * hd, (kv + 1) * hd
        kb = jnp.concatenate([kp_ref[:, lo:hi], kc_ref[:, lo:hi]], axis=0)
        vb = jnp.concatenate([vp_ref[:, lo:hi], vc_ref[:, lo:hi]], axis=0)
        for g in range(grp):
            hq = kv * grp + g
            qh = q_ref[:, hq * hd:(hq + 1) * hd]
            s = _dot_nt(qh, kb) * scale + bias_ref[hq]
            s = jnp.where(mask, s, NEG_INF)
            sink = sink_ref[hq]
            m = jnp.maximum(jnp.max(s, axis=-1, keepdims=True), sink)
            e = jnp.exp(s - m)
            p = e / (jnp.sum(e, axis=-1, keepdims=True) + jnp.exp(sink - m))
            outs.append(_dot(p.astype(BF16), vb))
    o_ref[...] = jnp.concatenate(outs, axis=1).astype(o_ref.dtype)


def swa_attention(qkv, sinks, rel_bias, batch, seq):
    n = qkv.shape[0]
    tq = SWA_QBLOCK
    nb = seq // tq
    qd = SWA_Q_HEADS * SWA_HEAD_DIM
    kcol = qd // LANES
    dist = np.arange(tq)[:, None] + tq - np.arange(2 * tq)[None, :]
    bias = rel_bias.astype(F32)[_t5_bucket(dist)].transpose(2, 0, 1)

    def cur(col):
        return pl.BlockSpec((tq, LANES), lambda b, i: (b * nb + i, col))

    def prev(col):
        return pl.BlockSpec((tq, LANES), lambda b, i: (b * nb + jnp.maximum(i - 1, 0), col))

    return pl.pallas_call(
        _swa_kernel,
        grid=(batch, nb),
        in_specs=[pl.BlockSpec(memory_space=pltpu.SMEM),
                  pl.BlockSpec((tq, qd), lambda b, i: (b * nb + i, 0)),
                  cur(kcol), prev(kcol), cur(kcol + 1), prev(kcol + 1),
                  pl.BlockSpec((SWA_Q_HEADS, tq, 2 * tq), lambda b, i: (0, 0, 0))],
        out_specs=pl.BlockSpec((tq, qd), lambda b, i: (b * nb + i, 0)),
        out_shape=jax.ShapeDtypeStruct((n, qd), BF16),
        compiler_params=_cparams(("parallel", "arbitrary"), 40),
        name="swa_attention",
    )(sinks.astype(F32), qkv, qkv, qkv, qkv, qkv, bias)


def _mla_qkv_kernel(c_ref, pos_ref, invf_ref, qn_ref, kvn_ref, wq_ref, wkv_ref, q_ref, k_ref, v_ref):
    nh, dn = MLA_HEADS, MLA_NOPE
    cq = c_ref[:, :MLA_Q_RANK].astype(F32)
    ckv = c_ref[:, MLA_Q_RANK:MLA_Q_RANK + MLA_KV_RANK].astype(F32)
    base = MLA_Q_RANK + MLA_KV_RANK
    kr_a = c_ref[:, base:base + LANES].astype(F32)
    kr_b = c_ref[:, base + LANES:base + 2 * LANES].astype(F32)
    ang = pos_ref[...].astype(F32) * invf_ref[...]
    cos, sin = jnp.cos(ang), jnp.sin(ang)
    scale = (MLA_NOPE + MLA_ROPE) ** -0.5 * math.log2(math.e)
    q = _dot(_rms(cq, qn_ref[...]).astype(BF16), wq_ref[...]) * scale
    kv = _dot(_rms(ckv, kvn_ref[...]).astype(BF16), wkv_ref[...])
    k_rot = (kr_a * cos + kr_b * sin).astype(BF16)
    for h in range(nh):
        q_nope = q[:, h * dn:(h + 1) * dn]
        q_a = q[:, (nh + h) * dn:(nh + h + 1) * dn]
        q_b = q[:, (2 * nh + h) * dn:(2 * nh + h + 1) * dn]
        q_ref[:, 2 * h * dn:(2 * h + 1) * dn] = q_nope.astype(BF16)
        q_ref[:, (2 * h + 1) * dn:(2 * h + 2) * dn] = (q_a * cos + q_b * sin).astype(BF16)
        k_ref[:, 2 * h * dn:(2 * h + 1) * dn] = kv[:, h * dn:(h + 1) * dn].astype(BF16)
        k_ref[:, (2 * h + 1) * dn:(2 * h + 2) * dn] = k_rot
    v_ref[...] = kv[:, nh * dn:].astype(BF16)


def _rotate_half_cols(w):
    half = w.shape[-1] // 2
    return jnp.concatenate([-w[..., half:], w[..., :half]], axis=-1)


def mla_qkv(c, positions, q_norm, w_q_b, kv_norm, w_kv_b, *, tm=512):
    n = c.shape[0]
    tm = min(tm, n)
    nh, dn, dr, dv = MLA_HEADS, MLA_NOPE, MLA_ROPE, MLA_V
    wq = w_q_b.reshape(MLA_Q_RANK, nh, dn + dr)
    pad = jnp.zeros((MLA_Q_RANK, nh, LANES - dr), w_q_b.dtype)
    wq_rope = wq[:, :, dn:]
    wq_all = jnp.concatenate([
        wq[:, :, :dn].reshape(MLA_Q_RANK, nh * dn),
        jnp.concatenate([wq_rope, pad], axis=-1).reshape(MLA_Q_RANK, nh * LANES),
        jnp.concatenate([_rotate_half_cols(wq_rope), pad], axis=-1).reshape(MLA_Q_RANK, nh * LANES),
    ], axis=1).astype(BF16)
    wkv = w_kv_b.reshape(MLA_KV_RANK, nh, dn + dv)
    wkv_all = jnp.concatenate([wkv[:, :, :dn].reshape(MLA_KV_RANK, nh * dn),
                               wkv[:, :, dn:].reshape(MLA_KV_RANK, nh * dv)], axis=1).astype(BF16)
    half = dr // 2
    inv_freq = ROPE_THETA ** (-jnp.arange(half, dtype=F32) / half)
    invf = jnp.concatenate([inv_freq, inv_freq, jnp.zeros((LANES - dr,), F32)]).reshape(1, LANES)
    cw = c.shape[1]
    return pl.pallas_call(
        _mla_qkv_kernel,
        grid=(n // tm,),
        in_specs=[pl.BlockSpec((tm, cw), lambda i: (i, 0)),
                  pl.BlockSpec((tm, 1), lambda i: (i, 0)),
                  pl.BlockSpec((1, LANES), lambda i: (0, 0)),
                  pl.BlockSpec((1, MLA_Q_RANK), lambda i: (0, 0)),
                  pl.BlockSpec((1, MLA_KV_RANK), lambda i: (0, 0)),
                  pl.BlockSpec(wq_all.shape, lambda i: (0, 0)),
                  pl.BlockSpec(wkv_all.shape, lambda i: (0, 0))],
        out_specs=[pl.BlockSpec((tm, 2 * nh * dn), lambda i: (i, 0)),
                   pl.BlockSpec((tm, 2 * nh * dn), lambda i: (i, 0)),
                   pl.BlockSpec((tm, nh * dv), lambda i: (i, 0))],
        out_shape=[jax.ShapeDtypeStruct((n, 2 * nh * dn), BF16),
                   jax.ShapeDtypeStruct((n, 2 * nh * dn), BF16),
                   jax.ShapeDtypeStruct((n, nh * dv), BF16)],
        compiler_params=_cparams(("parallel",), 48),
        name="mla_qkv",
    )(c, positions.reshape(n, 1), invf, q_norm.reshape(1, -1).astype(F32), kv_norm.reshape(1, -1).astype(F32),
      wq_all, wkv_all)


def _mla_attn_kernel(q_ref, k_ref, v_ref, o_ref, *, tk):
    i = pl.program_id(2)
    tq = q_ref.shape[0]
    dq = q_ref.shape[1] // MLA_HP
    dv = MLA_V
    row = i * tq + lax.broadcasted_iota(jnp.int32, (tq, tk), 0)
    col = lax.broadcasted_iota(jnp.int32, (tq, tk), 1)
    g = MLA_HP

    def heads(x, w):
        return jnp.stack([x[:, hh * w:(hh + 1) * w] for hh in range(g)])

    q = heads(q_ref[...], dq)

    def block(kb, state, masked):
        m, l, acc = state
        start = pl.multiple_of(kb * tk, tk)
        s = _bmm_nt(q, heads(k_ref[pl.ds(start, tk), :], dq))
        if masked:
            s = jnp.where(col + kb * tk <= row, s, NEG_INF)
        m_new = jnp.maximum(m, jnp.max(s, axis=-1, keepdims=True))
        alpha = jnp.exp2(m - m_new)
        p = jnp.exp2(s - m_new)
        l = alpha * l + jnp.sum(p, axis=-1, keepdims=True)
        acc = alpha * acc + _bmm(p.astype(BF16), heads(v_ref[pl.ds(start, tk), :], dv))
        return m_new, l, acc

    last = (i * tq + tq - 1) // tk
    state = (jnp.full((g, tq, 1), NEG_INF, F32), jnp.zeros((g, tq, 1), F32), jnp.zeros((g, tq, dv), F32))
    state = lax.fori_loop(0, last, lambda kb, st: block(kb, st, False), state)
    _, l, acc = block(last, state, True)
    out = acc / l
    o_ref[...] = jnp.concatenate([out[hh] for hh in range(g)], axis=1).astype(o_ref.dtype)


def mla_attention(qcat, kcat, v, batch, seq):
    n = qcat.shape[0]
    tq, tk = min(MLA_TQ, seq), min(MLA_TK, seq)
    assert tk % tq == 0
    nq = seq // tq
    dq = MLA_HP * qcat.shape[1] // MLA_HEADS
    dv = MLA_HP * MLA_V
    return pl.pallas_call(
        functools.partial(_mla_attn_kernel, tk=tk),
        grid=(batch, MLA_HEADS // MLA_HP, nq),
        in_specs=[pl.BlockSpec((tq, dq), lambda b, h, i: (b * nq + i, h)),
                  pl.BlockSpec((seq, dq), lambda b, h, i: (b, h)),
                  pl.BlockSpec((seq, dv), lambda b, h, i: (b, h))],
        out_specs=pl.BlockSpec((tq, dv), lambda b, h, i: (b * nq + i, h)),
        out_shape=jax.ShapeDtypeStruct((n, MLA_HEADS * MLA_V), BF16),
        compiler_params=_cparams(("parallel", "parallel", "arbitrary"), 40),
        name="mla_attention",
    )(qcat, kcat, v)


def _router_kernel(x_ref, nw_ref, wr_ref, xn_ref, meta_ref, cnt_ref):
    i = pl.program_id(0)
    tm = x_ref.shape[0]

    @pl.when(i == 0)
    def _():
        cnt_ref[...] = jnp.zeros_like(cnt_ref)

    xn = _rms(x_ref[...], nw_ref[...])
    xn_ref[...] = xn
    logits = _dot_hi(xn, wr_ref[...])
    lane = lax.broadcasted_iota(jnp.int32, logits.shape, 1).astype(F32)
    logits = jnp.where(lane < N_EXPERTS, logits, NEG_INF)
    m1 = jnp.max(logits, axis=-1, keepdims=True)
    e1 = jnp.min(jnp.where(logits == m1, lane, float(LANES)), axis=-1, keepdims=True)
    rest = jnp.where(lane == e1, NEG_INF, logits)
    m2 = jnp.max(rest, axis=-1, keepdims=True)
    e2 = jnp.min(jnp.where(rest == m2, lane, float(LANES)), axis=-1, keepdims=True)
    ex = jnp.exp(m2 - m1)
    w1 = 1.0 / (1.0 + ex)
    w2 = ex / (1.0 + ex)
    oh1 = jnp.where(lane == e1, 1.0, 0.0)
    oh2 = jnp.where(lane == e2, 1.0, 0.0)
    oh = (oh1 + oh2).astype(BF16)
    r = lax.broadcasted_iota(jnp.int32, (tm, tm), 0)
    c = lax.broadcasted_iota(jnp.int32, (tm, tm), 1)
    before = _dot(jnp.where(c < r, 1.0, 0.0).astype(BF16), oh)
    base = before + cnt_ref[0:1, :]
    rank1 = jnp.sum(oh1 * base, axis=-1, keepdims=True)
    rank2 = jnp.sum(oh2 * base, axis=-1, keepdims=True)
    cnt_ref[...] = cnt_ref[...] + jnp.sum(oh.astype(F32), axis=0, keepdims=True)
    ml = lax.broadcasted_iota(jnp.int32, meta_ref.shape, 1)
    meta = jnp.where(ml == 0, e1, 0.0)
    meta = jnp.where(ml == 1, e2, meta)
    meta = jnp.where(ml == 2, rank1, meta)
    meta = jnp.where(ml == 3, rank2, meta)
    meta = jnp.where(ml == 4, w1, meta)
    meta = jnp.where(ml == 5, w2, meta)
    meta_ref[...] = meta


def moe_router(h, nw, w_router, *, tm=512):
    m, d = h.shape
    tm = min(tm, m)
    wr = jnp.zeros((d, LANES), F32).at[:, :N_EXPERTS].set(w_router.astype(F32))
    return pl.pallas_call(
        _router_kernel,
        grid=(m // tm,),
        in_specs=[pl.BlockSpec((tm, d), lambda i: (i, 0)),
                  pl.BlockSpec((1, d), lambda i: (0, 0)),
                  pl.BlockSpec((d, LANES), lambda i: (0, 0))],
        out_specs=[pl.BlockSpec((tm, d), lambda i: (i, 0)),
                   pl.BlockSpec((tm, 8), lambda i: (i, 0)),
                   pl.BlockSpec((8, LANES), lambda i: (0, 0))],
        out_shape=[jax.ShapeDtypeStruct((m, d), F32),
                   jax.ShapeDtypeStruct((m, 8), F32),
                   jax.ShapeDtypeStruct((8, LANES), F32)],
        compiler_params=_cparams(("arbitrary",), 40),
        name="moe_router",
    )(h, nw.reshape(1, d), wr)


GATHER_UNROLL = 8


def _row_copy(src_hbm, src_row, dst_ref, r, sem):
    return pltpu.make_async_copy(src_hbm.at[pl.ds(src_row, 1)], dst_ref.at[pl.ds(r, 1)], sem)


def _gather_start(idx_hbm, chunk, src_hbm, dst_ref, idx_smem, sem_idx, sem_rows):
    n = dst_ref.shape[0]
    off = pl.multiple_of(chunk * n, n)
    cp = pltpu.make_async_copy(idx_hbm.at[pl.ds(off, n)], idx_smem, sem_idx)
    cp.start()
    cp.wait()

    def issue(blk, carry):
        base = pl.multiple_of(blk * GATHER_UNROLL, GATHER_UNROLL)
        for u in range(GATHER_UNROLL):
            _row_copy(src_hbm, idx_smem[base + u], dst_ref, base + u, sem_rows).start()
        return carry

    lax.fori_loop(0, n // GATHER_UNROLL, issue, 0)


def _gather_wait(src_hbm, dst_ref, sem_rows):
    def drain(r, carry):
        _row_copy(src_hbm, 0, dst_ref, r, sem_rows).wait()
        return carry

    lax.fori_loop(0, dst_ref.shape[0], drain, 0, unroll=GATHER_UNROLL)


def _moe_kernel(te_ref, nt_ref, idx_hbm, x_hbm, wg_ref, wu_ref, wd_ref, sw_ref, o_ref,
                xf_ref, xb_ref, acc_ref, idx_smem, sem_idx, sem_rows):
    i = pl.program_id(0)
    j = pl.program_id(1)
    n_used = nt_ref[0]

    @pl.when(i < n_used)
    def _():
        @pl.when(j == 0)
        def _():
            slot = i % 2
            @pl.when(i == 0)
            def _():
                _gather_start(idx_hbm, i, x_hbm, xf_ref.at[slot], idx_smem, sem_idx, sem_rows.at[slot])

            _gather_wait(x_hbm, xf_ref.at[slot], sem_rows.at[slot])

            @pl.when(i + 1 < n_used)
            def _():
                _gather_start(idx_hbm, i + 1, x_hbm, xf_ref.at[1 - slot], idx_smem, sem_idx, sem_rows.at[1 - slot])

            xb_ref[...] = xf_ref[slot].astype(BF16)
            acc_ref[...] = jnp.zeros_like(acc_ref)

        x = xb_ref[...]
        a = (_silu(_dot(x, wg_ref[0])) * _dot(x, wu_ref[0])).astype(BF16)
        acc_ref[...] += _dot(a, wd_ref[0])

    @pl.when(j == pl.num_programs(1) - 1)
    def _():
        o_ref[...] = acc_ref[...] * sw_ref[...]


def moe_experts(xn, sorted_tok, sorted_w, tile_expert, n_tiles_used, w_gate_up, w_down, *, tf=512):
    n, d = xn.shape
    ne, f, _ = w_down.shape
    tm = MOE_TILE
    p = sorted_tok.shape[0]
    n_tiles = p // tm
    tf = min(tf, f)
    nf = f // tf
    grid_spec = pltpu.PrefetchScalarGridSpec(
        num_scalar_prefetch=2,
        grid=(n_tiles, nf),
        in_specs=[pl.BlockSpec(memory_space=pl.ANY),
                  pl.BlockSpec(memory_space=pl.ANY),
                  pl.BlockSpec((1, d, tf), lambda i, j, te, nt: (te[i], 0, j)),
                  pl.BlockSpec((1, d, tf), lambda i, j, te, nt: (te[i], 0, j + nf)),
                  pl.BlockSpec((1, tf, d), lambda i, j, te, nt: (te[i], j, 0)),
                  pl.BlockSpec((tm, 1), lambda i, j, te, nt: (i, 0))],
        out_specs=pl.BlockSpec((tm, d), lambda i, j, te, nt: (i, 0)),
        scratch_shapes=[pltpu.VMEM((2, tm, d), F32), pltpu.VMEM((tm, d), BF16), pltpu.VMEM((tm, d), F32),
                        pltpu.SMEM((tm,), jnp.int32), pltpu.SemaphoreType.DMA, pltpu.SemaphoreType.DMA((2,))],
    )
    return pl.pallas_call(
        _moe_kernel,
        grid_spec=grid_spec,
        out_shape=jax.ShapeDtypeStruct((p, d), F32),
        compiler_params=_cparams(("arbitrary", "arbitrary"), 56),
        name="moe_experts",
    )(tile_expert, n_tiles_used, sorted_tok, xn, w_gate_up, w_gate_up, w_down, sorted_w.reshape(p, 1))


def _combine_kernel(idx_hbm, y_hbm, h_ref, nw_ref, o_ref, buf_ref, idx_smem, sem_idx, sem_rows, *, final_norm):
    i = pl.program_id(0)
    tc = h_ref.shape[0]
    slot = i % 2

    @pl.when(i == 0)
    def _():
        _gather_start(idx_hbm, i, y_hbm, buf_ref.at[slot], idx_smem, sem_idx, sem_rows.at[slot])

    _gather_wait(y_hbm, buf_ref.at[slot], sem_rows.at[slot])

    @pl.when(i + 1 < pl.num_programs(0))
    def _():
        _gather_start(idx_hbm, i + 1, y_hbm, buf_ref.at[1 - slot], idx_smem, sem_idx, sem_rows.at[1 - slot])

    out = h_ref[...] + buf_ref[slot, :tc, :] + buf_ref[slot, tc:, :]
    if final_norm:
        out = _rms(out, nw_ref[...])
    o_ref[...] = out


def moe_combine(h, y_sorted, pos_tiles, nw, *, final_norm):
    n, d = h.shape
    tc = MOE_TILE // TOP_K
    return pl.pallas_call(
        functools.partial(_combine_kernel, final_norm=final_norm),
        grid=(n // tc,),
        in_specs=[pl.BlockSpec(memory_space=pl.ANY),
                  pl.BlockSpec(memory_space=pl.ANY),
                  pl.BlockSpec((tc, d), lambda i: (i, 0)),
                  pl.BlockSpec((1, d), lambda i: (0, 0))],
        out_specs=pl.BlockSpec((tc, d), lambda i: (i, 0)),
        out_shape=jax.ShapeDtypeStruct((n, d), F32),
        scratch_shapes=[pltpu.VMEM((2, TOP_K * tc, d), F32), pltpu.SMEM((TOP_K * tc,), jnp.int32),
                        pltpu.SemaphoreType.DMA, pltpu.SemaphoreType.DMA((2,))],
        compiler_params=_cparams(("arbitrary",), 40),
        name="moe_combine",
    )(pos_tiles, y_sorted, h, nw.reshape(1, d))


def routed_swiglu(h, nw, w_router, w_gate_up, w_down, final_nw):
    n, d = h.shape
    tm = MOE_TILE
    tc = tm // TOP_K
    xn, meta, cnt = moe_router(h, nw, w_router)
    counts = cnt[0, :N_EXPERTS].astype(jnp.int32)
    tiles_per = (counts + tm - 1) // tm
    tile_end = jnp.cumsum(tiles_per)
    start = (tile_end - tiles_per) * tm
    e = meta[:, 0:2].astype(jnp.int32)
    pos = start[e] + meta[:, 2:4].astype(jnp.int32)
    p = TOP_K * n + N_EXPERTS * tm
    n_tiles = p // tm
    assign = jnp.full((p,), -1, jnp.int32).at[pos.reshape(-1)].set(
        jnp.arange(TOP_K * n, dtype=jnp.int32), unique_indices=True)
    valid = assign >= 0
    sorted_tok = jnp.where(valid, assign // TOP_K, 0)
    sorted_w = jnp.where(valid, meta[:, 4:6].reshape(-1)[jnp.maximum(assign, 0)], 0.0)
    tile_expert = jnp.minimum(jnp.searchsorted(tile_end, jnp.arange(n_tiles, dtype=jnp.int32), side="right"),
                              N_EXPERTS - 1).astype(jnp.int32)
    y = moe_experts(xn, sorted_tok, sorted_w, tile_expert, tile_end[-1:].astype(jnp.int32), w_gate_up, w_down)
    pos_tiles = pos.reshape(n // tc, tc, TOP_K).transpose(0, 2, 1).reshape(-1)
    return moe_combine(h, y, pos_tiles, final_nw if final_nw is not None else nw, final_norm=final_nw is not None)


def kernel(x, positions, rel_bias, attn_norm, ffn_norm, final_norm, gdn_w_in, gdn_conv_w, gdn_a_log, gdn_dt_bias,
           gdn_norm_w, gdn_w_out, sb_w_in, sb_w_out, swa_w_in, swa_sinks, swa_w_out, mla_w_in, mla_q_norm,
           mla_w_q_b, mla_kv_norm, mla_w_kv_b, mla_w_out, ffn_w_gate_up, ffn_w_down, moe_w_router, moe_w_gate_up,
           moe_w_down):
    batch, seq, d = x.shape
    n = batch * seq
    depth = attn_norm.shape[0]
    h = x.reshape(n, d).astype(F32)
    bf = lambda a: a.astype(BF16)
    for i in range(depth):
        mixer, occ = i % 4, i // 4
        if mixer == 0:
            w_in = gdn_w_in[occ]
            qkvz_dim = 4 * GDN_HEADS * GDN_HEAD_DIM
            qkvz = norm_matmul(h, attn_norm[i], bf(w_in[:, :qkvz_dim]), tn=1024)
            gcol, grow = gdn_gates(h, attn_norm[i], w_in[:, qkvz_dim:].astype(F32), gdn_a_log[occ], gdn_dt_bias[occ])
            o = gdn_core(qkvz, gdn_conv_w[occ].astype(F32), gcol, grow, gdn_norm_w[occ].astype(F32), batch, seq)
            h = matmul_residual(o, bf(gdn_w_out[occ]), h)
        elif mixer == 1:
            qkv = norm_matmul(h, attn_norm[i], bf(sb_w_in[occ]), tn=1024)
            o = sb_attention(qkv, batch, seq)
            h = matmul_residual(o, bf(sb_w_out[occ]), h)
        elif mixer == 2:
            qkv = norm_matmul(h, attn_norm[i], bf(swa_w_in[occ]), tn=256)
            o = swa_attention(qkv, swa_sinks[occ], rel_bias, batch, seq)
            h = matmul_residual(o, bf(swa_w_out[occ]), h)
        else:
            w_in = mla_w_in[occ]
            base = MLA_Q_RANK + MLA_KV_RANK
            kr = w_in[:, base:]
            zpad = jnp.zeros((d, LANES - MLA_ROPE), w_in.dtype)
            w_all = jnp.concatenate([w_in[:, :base], kr, zpad, _rotate_half_cols(kr), zpad], axis=1)
            c = norm_matmul(h, attn_norm[i], bf(w_all), tn=w_all.shape[1], out_dtype=F32)
            qcat, kcat, v = mla_qkv(c, positions, mla_q_norm[occ], mla_w_q_b[occ], mla_kv_norm[occ], mla_w_kv_b[occ])
            o = mla_attention(qcat, kcat, v, batch, seq)
            h = matmul_residual(o, bf(mla_w_out[occ]), h)
        f = i // 2
        last = i == depth - 1
        if i % 2 == 0:
            h = ffn_dense(h, ffn_norm[i], bf(ffn_w_gate_up[f]), bf(ffn_w_down[f]))
            if last:
                h = final_rmsnorm(h, final_norm)
        else:
            h = routed_swiglu(h, ffn_norm[i], moe_w_router[f], bf(moe_w_gate_up[f]), bf(moe_w_down[f]),
                              final_norm if last else None)
    return h.reshape(batch, seq, d).astype(x.dtype)


def _final_norm_kernel(x_ref, nw_ref, o_ref):
    o_ref[...] = _rms(x_ref[...], nw_ref[...])


def final_rmsnorm(h, nw, *, tm=512):
    m, d = h.shape
    tm = min(tm, m)
    return pl.pallas_call(
        _final_norm_kernel,
        grid=(m // tm,),
        in_specs=[pl.BlockSpec((tm, d), lambda i: (i, 0)), pl.BlockSpec((1, d), lambda i: (0, 0))],
        out_specs=pl.BlockSpec((tm, d), lambda i: (i, 0)),
        out_shape=jax.ShapeDtypeStruct((m, d), F32),
        compiler_params=_cparams(("parallel",), 40),
        name="final_rmsnorm",
    )(h, nw.reshape(1, d))
```

```python
import functools
import math

import numpy as np
import jax
import jax.numpy as jnp
from jax import lax
from jax.experimental import pallas as pl
from jax.experimental.pallas import tpu as pltpu

F32 = jnp.float32
BF16 = jnp.bfloat16
HIGHEST = lax.Precision.HIGHEST

NORM_EPS = 1e-6
NEG_INF = -1e30
LANES = 128

GDN_HEADS, GDN_HEAD_DIM, GDN_CONV = 8, 128, 4
GDN_BLOCK = 256
GDN_INV_BASE = 16
GDN_HP = 4
SB_HEADS, SB_HEAD_DIM, SB_BLOCK = 16, 64, 256
SB_HP = 8
SB_ZERO_EXP = 160.0
SWA_Q_HEADS, SWA_KV_HEADS, SWA_HEAD_DIM, SWA_WINDOW, SWA_QBLOCK = 16, 2, 64, 128, 128
REL_BUCKETS, REL_MAX_DIST = 32, 128
MLA_HEADS, MLA_Q_RANK, MLA_KV_RANK, MLA_NOPE, MLA_ROPE, MLA_V = 8, 384, 256, 128, 64, 128
MLA_HP = 2
MLA_TQ, MLA_TK = 512, 1024
ROPE_THETA = 10000.0
N_EXPERTS, TOP_K = 8, 2
MOE_TILE = 1024


def _cparams(sem, vmem_mb):
    return pltpu.CompilerParams(dimension_semantics=sem, vmem_limit_bytes=vmem_mb << 20)


def _dot(a, b):
    return jnp.dot(a, b, preferred_element_type=F32)


def _dot_nt(a, b):
    return lax.dot_general(a, b, (((1,), (1,)), ((), ())), preferred_element_type=F32)


def _bmm(a, b):
    return jnp.einsum("gmk,gkn->gmn", a, b, preferred_element_type=F32)


def _bmm_nt(a, b):
    return jnp.einsum("gmk,gnk->gmn", a, b, preferred_element_type=F32)


def _dot_hi(a, b):
    return jnp.dot(a, b, preferred_element_type=F32, precision=HIGHEST)


def _dot_nt_hi(a, b):
    return lax.dot_general(a, b, (((1,), (1,)), ((), ())), preferred_element_type=F32, precision=HIGHEST)


def _rms(x, w):
    return x * lax.rsqrt(jnp.mean(x * x, axis=-1, keepdims=True) + NORM_EPS) * w


def _silu(x):
    return x * jax.nn.sigmoid(x)


def _softplus(x):
    return jnp.maximum(x, 0.0) + jnp.log(1.0 + jnp.exp(-jnp.abs(x)))


def _norm_matmul_kernel(x_ref, nw_ref, w_ref, o_ref, xn_ref):
    @pl.when(pl.program_id(1) == 0)
    def _():
        xn_ref[...] = _rms(x_ref[...], nw_ref[...]).astype(BF16)

    o_ref[...] = _dot(xn_ref[...], w_ref[...]).astype(o_ref.dtype)


def norm_matmul(x, nw, w, *, tm=1024, tn=512, out_dtype=BF16):
    m, d = x.shape
    n = w.shape[1]
    tm, tn = min(tm, m), min(tn, n)
    return pl.pallas_call(
        _norm_matmul_kernel,
        grid=(m // tm, n // tn),
        in_specs=[pl.BlockSpec((tm, d), lambda i, j: (i, 0)),
                  pl.BlockSpec((1, d), lambda i, j: (0, 0)),
                  pl.BlockSpec((d, tn), lambda i, j: (0, j))],
        out_specs=pl.BlockSpec((tm, tn), lambda i, j: (i, j)),
        out_shape=jax.ShapeDtypeStruct((m, n), out_dtype),
        scratch_shapes=[pltpu.VMEM((tm, d), BF16)],
        compiler_params=_cparams(("parallel", "arbitrary"), 40),
        name="norm_matmul",
    )(x, nw.reshape(1, d), w)


def _matmul_residual_kernel(a_ref, w_ref, h_ref, o_ref):
    o_ref[...] = h_ref[...] + _dot(a_ref[...], w_ref[...])


def matmul_residual(a, w, h, *, tm=512):
    m, k = a.shape
    d = w.shape[1]
    tm = min(tm, m)
    return pl.pallas_call(
        _matmul_residual_kernel,
        grid=(m // tm,),
        in_specs=[pl.BlockSpec((tm, k), lambda i: (i, 0)),
                  pl.BlockSpec((k, d), lambda i: (0, 0)),
                  pl.BlockSpec((tm, d), lambda i: (i, 0))],
        out_specs=pl.BlockSpec((tm, d), lambda i: (i, 0)),
        out_shape=jax.ShapeDtypeStruct((m, d), F32),
        compiler_params=_cparams(("parallel",), 40),
        name="matmul_residual",
    )(a, w, h)


def _ffn_kernel(x_ref, nw_ref, wg_ref, wu_ref, wd_ref, o_ref, xn_ref, acc_ref):
    j = pl.program_id(1)

    @pl.when(j == 0)
    def _():
        x = x_ref[...]
        xn_ref[...] = _rms(x, nw_ref[...]).astype(BF16)
        acc_ref[...] = x

    xn = xn_ref[...]
    a = (_silu(_dot(xn, wg_ref[...])) * _dot(xn, wu_ref[...])).astype(BF16)
    acc_ref[...] += _dot(a, wd_ref[...])

    @pl.when(j == pl.num_programs(1) - 1)
    def _():
        o_ref[...] = acc_ref[...]


def ffn_dense(h, nw, w_gate_up, w_down, *, tm=1024, tf=256):
    m, d = h.shape
    f = w_down.shape[0]
    tm, tf = min(tm, m), min(tf, f)
    nf = f // tf
    return pl.pallas_call(
        _ffn_kernel,
        grid=(m // tm, nf),
        in_specs=[pl.BlockSpec((tm, d), lambda i, j: (i, 0)),
                  pl.BlockSpec((1, d), lambda i, j: (0, 0)),
                  pl.BlockSpec((d, tf), lambda i, j: (0, j)),
                  pl.BlockSpec((d, tf), lambda i, j: (0, j + nf)),
                  pl.BlockSpec((tf, d), lambda i, j: (j, 0))],
        out_specs=pl.BlockSpec((tm, d), lambda i, j: (i, 0)),
        out_shape=jax.ShapeDtypeStruct((m, d), F32),
        scratch_shapes=[pltpu.VMEM((tm, d), BF16), pltpu.VMEM((tm, d), F32)],
        compiler_params=_cparams(("parallel", "arbitrary"), 48),
        name="ffn_dense",
    )(h, nw.reshape(1, d), w_gate_up, w_gate_up, w_down)


def _gdn_gates_kernel(x_ref, nw_ref, w_ref, pcol_ref, col_ref, row_ref):
    tm = x_ref.shape[0]
    nh = GDN_HEADS
    xn = _rms(x_ref[...], nw_ref[...])
    lc = _dot_hi(xn, w_ref[...])
    ci = lax.broadcasted_iota(jnp.int32, lc.shape, 1)
    is_g_c = (ci >= nh) & (ci < 2 * nh)
    g_c = jnp.where(is_g_c, pcol_ref[0:1, :] * _softplus(lc + pcol_ref[1:2, :]), 0.0)
    r = lax.broadcasted_iota(jnp.int32, (tm, tm), 0)
    c = lax.broadcasted_iota(jnp.int32, (tm, tm), 1)
    lower = jnp.where(((r // GDN_BLOCK) == (c // GDN_BLOCK)) & (c <= r), 1.0, 0.0).astype(F32)
    col = jnp.where(ci < nh, jax.nn.sigmoid(lc), _dot_hi(lower, g_c))
    col_ref[...] = col
    row_ref[...] = col.T[:2 * nh]


def gdn_gates(h, nw, w_bd, a_log, dt_bias, *, tm=512):
    m, d = h.shape
    tm = min(tm, m)
    nh = GDN_HEADS
    w_pad = jnp.zeros((d, LANES), F32).at[:, :2 * nh].set(w_bd)
    z = jnp.zeros((nh,), F32)
    prm = jnp.stack([jnp.concatenate([z, -jnp.exp(a_log.astype(F32))]),
                     jnp.concatenate([z, dt_bias.astype(F32)])])
    pcol = jnp.zeros((8, LANES), F32).at[:2, :2 * nh].set(prm)
    return pl.pallas_call(
        _gdn_gates_kernel,
        grid=(m // tm,),
        in_specs=[pl.BlockSpec((tm, d), lambda i: (i, 0)),
                  pl.BlockSpec((1, d), lambda i: (0, 0)),
                  pl.BlockSpec((d, LANES), lambda i: (0, 0)),
                  pl.BlockSpec((8, LANES), lambda i: (0, 0))],
        out_specs=[pl.BlockSpec((tm, LANES), lambda i: (i, 0)),
                   pl.BlockSpec((2 * nh, tm), lambda i: (0, i))],
        out_shape=[jax.ShapeDtypeStruct((m, LANES), F32), jax.ShapeDtypeStruct((2 * nh, m), F32)],
        compiler_params=_cparams(("parallel",), 40),
        name="gdn_gates",
    )(h, nw.reshape(1, d), w_pad, pcol)


def _unit_lower_inverse(low, eye, rc_xor):
    n = low.shape[-1]
    b = GDN_INV_BASE
    nb = jnp.where(rc_xor < b, -low, 0.0)
    nb_b = nb.astype(BF16)
    n2 = _bmm(nb_b, nb_b)
    n2_b = n2.astype(BF16)
    n4 = _bmm(n2_b, n2_b)
    n4_b = n4.astype(BF16)
    n8 = _bmm(n4_b, n4_b)
    p_off = nb + n2 + _bmm(nb_b, n2_b)
    q = eye + n4 + n8 + _bmm(n4_b, n8.astype(BF16))
    inv = q + _bmm(p_off.astype(BF16), q.astype(BF16))
    while b < n:
        m = jnp.where((rc_xor >> int(math.log2(b))) == 1, low, 0.0).astype(BF16)
        inv_b = inv.astype(BF16)
        inv = inv - _bmm(inv_b, _bmm(m, inv_b).astype(BF16))
        b *= 2
    return inv


def _gdn_core_kernel(q_ref, k_ref, v_ref, z_ref, cwq_ref, cwk_ref, cwv_ref, col_ref, row_ref, nw_ref,
                     o_ref, s_ref, tail_ref, ext_ref):
    hp = pl.program_id(1)
    t = pl.program_id(2)
    tb = q_ref.shape[0]
    dk = GDN_HEAD_DIM
    nh = GDN_HEADS

    @pl.when(t == 0)
    def _():
        s_ref[...] = jnp.zeros_like(s_ref)
        tail_ref[...] = jnp.zeros_like(tail_ref)

    r = lax.broadcasted_iota(jnp.int32, (tb, tb), 0)
    c = lax.broadcasted_iota(jnp.int32, (tb, tb), 1)
    rc_xor = jnp.bitwise_xor(r, c)
    eye = jnp.where(r == c, 1.0, 0.0).astype(F32)
    incl = c <= r
    strict = c < r
    colv = col_ref[...]
    ci = lax.broadcasted_iota(jnp.int32, colv.shape, 1)

    def conv_silu(slot, x_ref, cw_ref, lanes):
        x = x_ref[:, lanes].astype(F32)
        ext_ref[slot, 0:8, :] = tail_ref[slot]
        ext_ref[slot, 8:, :] = x
        tail_ref[slot] = x[tb - 8:, :]
        acc = x * cw_ref[GDN_CONV - 1:GDN_CONV, lanes]
        for s in range(1, GDN_CONV):
            acc = acc + ext_ref[slot, 8 - s:8 - s + tb, :] * cw_ref[GDN_CONV - 1 - s:GDN_CONV - s, lanes]
        return _silu(acc)

    per_head = []
    for j in range(GDN_HP):
        h = hp * GDN_HP + j
        lanes = slice(j * dk, (j + 1) * dk)
        q = conv_silu(3 * j, q_ref, cwq_ref, lanes)
        k = conv_silu(3 * j + 1, k_ref, cwk_ref, lanes)
        v = conv_silu(3 * j + 2, v_ref, cwv_ref, lanes)
        qn = q * (lax.rsqrt(jnp.sum(q * q, axis=-1, keepdims=True) + NORM_EPS) * dk ** -0.5)
        kn = k * lax.rsqrt(jnp.sum(k * k, axis=-1, keepdims=True) + NORM_EPS)
        beta = jnp.sum(jnp.where(ci == h, colv, 0.0), axis=1, keepdims=True)
        gc = jnp.sum(jnp.where(ci == h + nh, colv, 0.0), axis=1, keepdims=True)
        gr = row_ref[pl.ds(h + nh, 1), :]
        per_head.append((qn, kn, v, beta, gc, gr))

    qn, kn, v, beta, gc, gr = (jnp.stack(z) for z in zip(*per_head))
    g_last = gr[:, :, tb - 1:tb]
    kn_b = kn.astype(BF16)
    decay = jnp.where(incl, jnp.exp(jnp.where(incl, gc - gr, 0.0)), 0.0)
    kb = kn * beta
    low = jnp.where(strict, _bmm_nt(kb.astype(BF16), kn_b) * decay, 0.0)
    tinv = _unit_lower_inverse(low, eye, rc_xor)
    rhs = jnp.concatenate([v * beta, kb * jnp.exp(gc)], axis=2)
    uw = _bmm(tinv.astype(BF16), rhs.astype(BF16))
    u, w = uw[:, :, :dk], uw[:, :, dk:]
    attn = (_bmm_nt(qn.astype(BF16), kn_b) * decay).astype(BF16)

    s = s_ref[...]
    sb = s.astype(BF16)
    v_new = u - _bmm(w.astype(BF16), sb)
    v_new_b = v_new.astype(BF16)
    o = _bmm((qn * jnp.exp(gc)).astype(BF16), sb) + _bmm(attn, v_new_b)
    k_dec = kn * jnp.exp(g_last - gc)
    k_dec_t = jnp.stack([k_dec[j].T for j in range(GDN_HP)]).astype(BF16)
    s_ref[...] = s * jnp.exp(g_last) + _bmm(k_dec_t, v_new_b)

    for j in range(GDN_HP):
        lanes = slice(j * dk, (j + 1) * dk)
        o_ref[:, lanes] = (_rms(o[j], nw_ref[...]) * _silu(z_ref[:, lanes].astype(F32))).astype(o_ref.dtype)


def gdn_core(qkvz, conv_w, gcol, grow, norm_w, batch, seq):
    n = qkvz.shape[0]
    nh, dk, tb, hp = GDN_HEADS, GDN_HEAD_DIM, min(GDN_BLOCK, seq), GDN_HP
    nt = seq // tb
    ng = nh // hp

    def tok(sec):
        return pl.BlockSpec((tb, hp * dk), lambda b, g, t: (b * nt + t, g + sec * ng))

    def cw(sec):
        return pl.BlockSpec((GDN_CONV, hp * dk), lambda b, g, t: (0, g + sec * ng))

    return pl.pallas_call(
        _gdn_core_kernel,
        grid=(batch, ng, nt),
        in_specs=[tok(0), tok(1), tok(2), tok(3), cw(0), cw(1), cw(2),
                  pl.BlockSpec((tb, LANES), lambda b, g, t: (b * nt + t, 0)),
                  pl.BlockSpec((2 * nh, tb), lambda b, g, t: (0, b * nt + t)),
                  pl.BlockSpec((1, dk), lambda b, g, t: (0, 0))],
        out_specs=pl.BlockSpec((tb, hp * dk), lambda b, g, t: (b * nt + t, g)),
        out_shape=jax.ShapeDtypeStruct((n, nh * dk), BF16),
        scratch_shapes=[pltpu.VMEM((hp, dk, dk), F32), pltpu.VMEM((3 * hp, 8, dk), F32),
                        pltpu.VMEM((3 * hp, tb + 8, dk), F32)],
        compiler_params=_cparams(("parallel", "parallel", "arbitrary"), 40),
        name="gdn_core",
    )(qkvz, qkvz, qkvz, qkvz, conv_w, conv_w, conv_w, gcol, grow, norm_w.reshape(1, dk))


def _sb_attn_kernel(q_ref, k_ref, v_ref, o_ref, kmax_ref):
    i = pl.program_id(2)
    tq = q_ref.shape[0]
    hd = SB_HEAD_DIM
    nh = SB_HP
    log2e = math.log2(math.e)
    r = lax.broadcasted_iota(jnp.int32, (tq, tq), 0)
    c = lax.broadcasted_iota(jnp.int32, (tq, tq), 1)
    upper = jnp.where(r >= c, 1.0, 0.0).astype(BF16)
    strict = c < r
    def heads(x):
        return jnp.stack([x[:, hh * hd:(hh + 1) * hd] for hh in range(nh)])

    qh = heads((q_ref[...].astype(F32) * (hd ** -0.5 * log2e)).astype(BF16))

    @pl.when(i == 0)
    def _():
        lane = lax.broadcasted_iota(jnp.int32, (nh * hd, LANES), 0)
        col = lax.broadcasted_iota(jnp.int32, (nh * hd, LANES), 1)
        ind = jnp.where(lane // hd == col, 1.0, 0.0).astype(BF16)

        def chunk_max(cb, m):
            kf = k_ref[pl.ds(pl.multiple_of(cb * tq, tq), tq), :].astype(F32)
            return jnp.maximum(m, _dot((kf * kf).astype(BF16), ind))

        m = lax.fori_loop(0, k_ref.shape[0] // tq, chunk_max, jnp.zeros((tq, LANES), F32))
        kmax_ref[...] = jnp.broadcast_to(jnp.max(jnp.max(m, axis=0, keepdims=True), axis=1, keepdims=True),
                                         kmax_ref.shape)

    qf = qh.astype(F32)
    q2max = jnp.max(jnp.max(jnp.sum(qf * qf, axis=-1, keepdims=True), axis=0), axis=0, keepdims=True)
    w_bound = jnp.sqrt(q2max * kmax_ref[0:1, 0:1]) * 1.02

    def block(kb, state, masked):
        carry, acc = state
        start = pl.multiple_of(kb * tq, tq)
        kblk = heads(k_ref[pl.ds(start, tq), :])
        vblk = heads(v_ref[pl.ds(start, tq), :])
        w = jnp.einsum("hqd,hkd->hqk", qh, kblk, preferred_element_type=F32)
        sp = jnp.maximum(w, 0.0) + jnp.log(1.0 + jnp.exp2(-jnp.abs(w))) * log2e
        if masked:
            sp = jnp.where(strict, sp, 0.0)
        rloc = _dot(sp.astype(BF16).reshape(nh * tq, tq), upper).reshape(nh, tq, tq)
        a = jnp.exp2(w - rloc - carry)
        if masked:
            a = jnp.where(strict, a, 0.0)
        acc = acc + jnp.einsum("hqk,hkd->hqd", a.astype(BF16), vblk, preferred_element_type=F32)
        return carry + rloc[:, :, 0:1], acc

    def slack(carry):
        return jnp.sum(jnp.min(jnp.min(carry, axis=0), axis=0, keepdims=True) - w_bound)

    state = (jnp.zeros((nh, tq, 1), F32), jnp.zeros((nh, tq, hd), F32))
    carry, acc = block(i, state, True)

    def cond(st):
        return (st[0] < i) & (st[3] < SB_ZERO_EXP)

    def body(st):
        n, carry, acc, _ = st
        carry, acc = block(i - 1 - n, (carry, acc), False)
        return n + 1, carry, acc, slack(carry)

    _, _, acc, _ = lax.while_loop(cond, body, (jnp.int32(0), carry, acc, slack(carry)))
    o_ref[...] = jnp.concatenate([acc[hh] for hh in range(nh)], axis=1).astype(o_ref.dtype)


def sb_attention(qkv, batch, seq):
    n = qkv.shape[0]
    tq = min(SB_BLOCK, seq)
    nq = seq // tq
    gw = SB_HP * SB_HEAD_DIM
    npair = SB_HEADS // SB_HP
    return pl.pallas_call(
        _sb_attn_kernel,
        grid=(batch, npair, nq),
        in_specs=[pl.BlockSpec((tq, gw), lambda b, p, i: (b * nq + i, p)),
                  pl.BlockSpec((seq, gw), lambda b, p, i: (b, npair + p)),
                  pl.BlockSpec((seq, gw), lambda b, p, i: (b, 2 * npair + p))],
        out_specs=pl.BlockSpec((tq, gw), lambda b, p, i: (b * nq + i, p)),
        out_shape=jax.ShapeDtypeStruct((n, SB_HEADS * SB_HEAD_DIM), BF16),
        scratch_shapes=[pltpu.VMEM((8, LANES), F32)],
        compiler_params=_cparams(("parallel", "parallel", "arbitrary"), 40),
        name="sb_attention",
    )(qkv, qkv, qkv)


def _t5_bucket(dist):
    exact = REL_BUCKETS // 2
    n = np.maximum(dist, 0)
    log_ratio = (np.log(np.maximum(n, 1).astype(np.float32) / exact)
                 / np.log(np.float32(REL_MAX_DIST / exact)))
    large = np.minimum(exact + (log_ratio * (REL_BUCKETS - exact)).astype(np.int32), REL_BUCKETS - 1)
    return np.where(n < exact, n, large).astype(np.int32)


def _swa_kernel(sink_ref, q_ref, kc_ref, kp_ref, vc_ref, vp_ref, bias_ref, o_ref):
    nblk = pl.program_id(1)
    tq = q_ref.shape[0]
    hd = SWA_HEAD_DIM
    grp = SWA_Q_HEADS // SWA_KV_HEADS
    scale = hd ** -0.5
    r = lax.broadcasted_iota(jnp.int32, (tq, 2 * tq), 0)
    c = lax.broadcasted_iota(jnp.int32, (tq, 2 * tq), 1)
    dist = r + tq - c
    mask = (dist >= 0) & (dist < SWA_WINDOW) & ((c >= tq) | (nblk > 0))
    outs = []
    for kv in range(SWA_KV_HEADS):
        lo, hi = kv * hd, (kv + 1) * hd
        kb = jnp.concatenate([kp_ref[:, lo:hi], kc_ref[:, lo:hi]], axis=0)
        vb = jnp.concatenate([vp_ref[:, lo:hi], vc_ref[:, lo:hi]], axis=0)
        for g in range(grp):
            hq = kv * grp + g
            qh = q_ref[:, hq * hd:(hq + 1) * hd]
            s = _dot_nt(qh, kb) * scale + bias_ref[hq]
            s = jnp.where(mask, s, NEG_INF)
            sink = sink_ref[hq]
            m = jnp.maximum(jnp.max(s, axis=-1, keepdims=True), sink)
            e = jnp.exp(s - m)
            p = e / (jnp.sum(e, axis=-1, keepdims=True) + jnp.exp(sink - m))
            outs.append(_dot(p.astype(BF16), vb))
    o_ref[...] = jnp.concatenate(outs, axis=1).astype(o_ref.dtype)


def swa_attention(qkv, sinks, rel_bias, batch, seq):
    n = qkv.shape[0]
    tq = SWA_QBLOCK
    nb = seq // tq
    qd = SWA_Q_HEADS * SWA_HEAD_DIM
    kcol = qd // LANES
    dist = np.arange(tq)[:, None] + tq - np.arange(2 * tq)[None, :]
    bias = rel_bias.astype(F32)[_t5_bucket(dist)].transpose(2, 0, 1)

    def cur(col):
        return pl.BlockSpec((tq, LANES), lambda b, i: (b * nb + i, col))

    def prev(col):
        return pl.BlockSpec((tq, LANES), lambda b, i: (b * nb + jnp.maximum(i - 1, 0), col))

    return pl.pallas_call(
        _swa_kernel,
        grid=(batch, nb),
        in_specs=[pl.BlockSpec(memory_space=pltpu.SMEM),
                  pl.BlockSpec((tq, qd), lambda b, i: (b * nb + i, 0)),
                  cur(kcol), prev(kcol), cur(kcol + 1), prev(kcol + 1),
                  pl.BlockSpec((SWA_Q_HEADS, tq, 2 * tq), lambda b, i: (0, 0, 0))],
        out_specs=pl.BlockSpec((tq, qd), lambda b, i: (b * nb + i, 0)),
        out_shape=jax.ShapeDtypeStruct((n, qd), BF16),
        compiler_params=_cparams(("parallel", "arbitrary"), 40),
        name="swa_attention",
    )(sinks.astype(F32), qkv, qkv, qkv, qkv, qkv, bias)


def _mla_qkv_kernel(c_ref, pos_ref, invf_ref, qn_ref, kvn_ref, wq_ref, wkv_ref, q_ref, k_ref, v_ref):
    nh, dn = MLA_HEADS, MLA_NOPE
    cq = c_ref[:, :MLA_Q_RANK].astype(F32)
    ckv = c_ref[:, MLA_Q_RANK:MLA_Q_RANK + MLA_KV_RANK].astype(F32)
    base = MLA_Q_RANK + MLA_KV_RANK
    kr_a = c_ref[:, base:base + LANES].astype(F32)
    kr_b = c_ref[:, base + LANES:base + 2 * LANES].astype(F32)
    ang = pos_ref[...].astype(F32) * invf_ref[...]
    cos, sin = jnp.cos(ang), jnp.sin(ang)
    scale = (MLA_NOPE + MLA_ROPE) ** -0.5 * math.log2(math.e)
    q = _dot(_rms(cq, qn_ref[...]).astype(BF16), wq_ref[...]) * scale
    kv = _dot(_rms(ckv, kvn_ref[...]).astype(BF16), wkv_ref[...])
    k_rot = (kr_a * cos + kr_b * sin).astype(BF16)
    for h in range(nh):
        q_nope = q[:, h * dn:(h + 1) * dn]
        q_a = q[:, (nh + h) * dn:(nh + h + 1) * dn]
        q_b = q[:, (2 * nh + h) * dn:(2 * nh + h + 1) * dn]
        q_ref[:, 2 * h * dn:(2 * h + 1) * dn] = q_nope.astype(BF16)
        q_ref[:, (2 * h + 1) * dn:(2 * h + 2) * dn] = (q_a * cos + q_b * sin).astype(BF16)
        k_ref[:, 2 * h * dn:(2 * h + 1) * dn] = kv[:, h * dn:(h + 1) * dn].astype(BF16)
        k_ref[:, (2 * h + 1) * dn:(2 * h + 2) * dn] = k_rot
    v_ref[...] = kv[:, nh * dn:].astype(BF16)


def _rotate_half_cols(w):
    half = w.shape[-1] // 2
    return jnp.concatenate([-w[..., half:], w[..., :half]], axis=-1)


def mla_qkv(c, positions, q_norm, w_q_b, kv_norm, w_kv_b, *, tm=512):
    n = c.shape[0]
    tm = min(tm, n)
    nh, dn, dr, dv = MLA_HEADS, MLA_NOPE, MLA_ROPE, MLA_V
    wq = w_q_b.reshape(MLA_Q_RANK, nh, dn + dr)
    pad = jnp.zeros((MLA_Q_RANK, nh, LANES - dr), w_q_b.dtype)
    wq_rope = wq[:, :, dn:]
    wq_all = jnp.concatenate([
        wq[:, :, :dn].reshape(MLA_Q_RANK, nh * dn),
        jnp.concatenate([wq_rope, pad], axis=-1).reshape(MLA_Q_RANK, nh * LANES),
        jnp.concatenate([_rotate_half_cols(wq_rope), pad], axis=-1).reshape(MLA_Q_RANK, nh * LANES),
    ], axis=1).astype(BF16)
    wkv = w_kv_b.reshape(MLA_KV_RANK, nh, dn + dv)
    wkv_all = jnp.concatenate([wkv[:, :, :dn].reshape(MLA_KV_RANK, nh * dn),
                               wkv[:, :, dn:].reshape(MLA_KV_RANK, nh * dv)], axis=1).astype(BF16)
    half = dr // 2
    inv_freq = ROPE_THETA ** (-jnp.arange(half, dtype=F32) / half)
    invf = jnp.concatenate([inv_freq, inv_freq, jnp.zeros((LANES - dr,), F32)]).reshape(1, LANES)
    cw = c.shape[1]
    return pl.pallas_call(
        _mla_qkv_kernel,
        grid=(n // tm,),
        in_specs=[pl.BlockSpec((tm, cw), lambda i: (i, 0)),
                  pl.BlockSpec((tm, 1), lambda i: (i, 0)),
                  pl.BlockSpec((1, LANES), lambda i: (0, 0)),
                  pl.BlockSpec((1, MLA_Q_RANK), lambda i: (0, 0)),
                  pl.BlockSpec((1, MLA_KV_RANK), lambda i: (0, 0)),
                  pl.BlockSpec(wq_all.shape, lambda i: (0, 0)),
                  pl.BlockSpec(wkv_all.shape, lambda i: (0, 0))],
        out_specs=[pl.BlockSpec((tm, 2 * nh * dn), lambda i: (i, 0)),
                   pl.BlockSpec((tm, 2 * nh * dn), lambda i: (i, 0)),
                   pl.BlockSpec((tm, nh * dv), lambda i: (i, 0))],
        out_shape=[jax.ShapeDtypeStruct((n, 2 * nh * dn), BF16),
                   jax.ShapeDtypeStruct((n, 2 * nh * dn), BF16),
                   jax.ShapeDtypeStruct((n, nh * dv), BF16)],
        compiler_params=_cparams(("parallel",), 48),
        name="mla_qkv",
    )(c, positions.reshape(n, 1), invf, q_norm.reshape(1, -1).astype(F32), kv_norm.reshape(1, -1).astype(F32),
      wq_all, wkv_all)


def _mla_attn_kernel(q_ref, k_ref, v_ref, o_ref, *, tk):
    i = pl.program_id(2)
    tq = q_ref.shape[0]
    dq = q_ref.shape[1] // MLA_HP
    dv = MLA_V
    row = i * tq + lax.broadcasted_iota(jnp.int32, (tq, tk), 0)
    col = lax.broadcasted_iota(jnp.int32, (tq, tk), 1)
    g = MLA_HP

    def heads(x, w):
        return jnp.stack([x[:, hh * w:(hh + 1) * w] for hh in range(g)])

    q = heads(q_ref[...], dq)

    def block(kb, state, masked):
        m, l, acc = state
        start = pl.multiple_of(kb * tk, tk)
        s = _bmm_nt(q, heads(k_ref[pl.ds(start, tk), :], dq))
        if masked:
            s = jnp.where(col + kb * tk <= row, s, NEG_INF)
        m_new = jnp.maximum(m, jnp.max(s, axis=-1, keepdims=True))
        alpha = jnp.exp2(m - m_new)
        p = jnp.exp2(s - m_new)
        l = alpha * l + jnp.sum(p, axis=-1, keepdims=True)
        acc = alpha * acc + _bmm(p.astype(BF16), heads(v_ref[pl.ds(start, tk), :], dv))
        return m_new, l, acc

    last = (i * tq + tq - 1) // tk
    state = (jnp.full((g, tq, 1), NEG_INF, F32), jnp.zeros((g, tq, 1), F32), jnp.zeros((g, tq, dv), F32))
    state = lax.fori_loop(0, last, lambda kb, st: block(kb, st, False), state)
    _, l, acc = block(last, state, True)
    out = acc / l
    o_ref[...] = jnp.concatenate([out[hh] for hh in range(g)], axis=1).astype(o_ref.dtype)


def mla_attention(qcat, kcat, v, batch, seq):
    n = qcat.shape[0]
    tq, tk = min(MLA_TQ, seq), min(MLA_TK, seq)
    assert tk % tq == 0
    nq = seq // tq
    dq = MLA_HP * qcat.shape[1] // MLA_HEADS
    dv = MLA_HP * MLA_V
    return pl.pallas_call(
        functools.partial(_mla_attn_kernel, tk=tk),
        grid=(batch, MLA_HEADS // MLA_HP, nq),
        in_specs=[pl.BlockSpec((tq, dq), lambda b, h, i: (b * nq + i, h)),
                  pl.BlockSpec((seq, dq), lambda b, h, i: (b, h)),
                  pl.BlockSpec((seq, dv), lambda b, h, i: (b, h))],
        out_specs=pl.BlockSpec((tq, dv), lambda b, h, i: (b * nq + i, h)),
        out_shape=jax.ShapeDtypeStruct((n, MLA_HEADS * MLA_V), BF16),
        compiler_params=_cparams(("parallel", "parallel", "arbitrary"), 40),
        name="mla_attention",
    )(qcat, kcat, v)


def _router_kernel(x_ref, nw_ref, wr_ref, xn_ref, meta_ref, cnt_ref):
    i = pl.program_id(0)
    tm = x_ref.shape[0]

    @pl.when(i == 0)
    def _():
        cnt_ref[...] = jnp.zeros_like(cnt_ref)

    xn = _rms(x_ref[...], nw_ref[...])
    xn_ref[...] = xn
    logits = _dot_hi(xn, wr_ref[...])
    lane = lax.broadcasted_iota(jnp.int32, logits.shape, 1).astype(F32)
    logits = jnp.where(lane < N_EXPERTS, logits, NEG_INF)
    m1 = jnp.max(logits, axis=-1, keepdims=True)
    e1 = jnp.min(jnp.where(logits == m1, lane, float(LANES)), axis=-1, keepdims=True)
    rest = jnp.where(lane == e1, NEG_INF, logits)
    m2 = jnp.max(rest, axis=-1, keepdims=True)
    e2 = jnp.min(jnp.where(rest == m2, lane, float(LANES)), axis=-1, keepdims=True)
    ex = jnp.exp(m2 - m1)
    w1 = 1.0 / (1.0 + ex)
    w2 = ex / (1.0 + ex)
    oh1 = jnp.where(lane == e1, 1.0, 0.0)
    oh2 = jnp.where(lane == e2, 1.0, 0.0)
    oh = (oh1 + oh2).astype(BF16)
    r = lax.broadcasted_iota(jnp.int32, (tm, tm), 0)
    c = lax.broadcasted_iota(jnp.int32, (tm, tm), 1)
    before = _dot(jnp.where(c < r, 1.0, 0.0).astype(BF16), oh)
    base = before + cnt_ref[0:1, :]
    rank1 = jnp.sum(oh1 * base, axis=-1, keepdims=True)
    rank2 = jnp.sum(oh2 * base, axis=-1, keepdims=True)
    cnt_ref[...] = cnt_ref[...] + jnp.sum(oh.astype(F32), axis=0, keepdims=True)
    ml = lax.broadcasted_iota(jnp.int32, meta_ref.shape, 1)
    meta = jnp.where(ml == 0, e1, 0.0)
    meta = jnp.where(ml == 1, e2, meta)
    meta = jnp.where(ml == 2, rank1, meta)
    meta = jnp.where(ml == 3, rank2, meta)
    meta = jnp.where(ml == 4, w1, meta)
    meta = jnp.where(ml == 5, w2, meta)
    meta_ref[...] = meta


def moe_router(h, nw, w_router, *, tm=512):
    m, d = h.shape
    tm = min(tm, m)
    wr = jnp.zeros((d, LANES), F32).at[:, :N_EXPERTS].set(w_router.astype(F32))
    return pl.pallas_call(
        _router_kernel,
        grid=(m // tm,),
        in_specs=[pl.BlockSpec((tm, d), lambda i: (i, 0)),
                  pl.BlockSpec((1, d), lambda i: (0, 0)),
                  pl.BlockSpec((d, LANES), lambda i: (0, 0))],
        out_specs=[pl.BlockSpec((tm, d), lambda i: (i, 0)),
                   pl.BlockSpec((tm, 8), lambda i: (i, 0)),
                   pl.BlockSpec((8, LANES), lambda i: (0, 0))],
        out_shape=[jax.ShapeDtypeStruct((m, d), F32),
                   jax.ShapeDtypeStruct((m, 8), F32),
                   jax.ShapeDtypeStruct((8, LANES), F32)],
        compiler_params=_cparams(("arbitrary",), 40),
        name="moe_router",
    )(h, nw.reshape(1, d), wr)


GATHER_UNROLL = 8


def _row_copy(src_hbm, src_row, dst_ref, r, sem):
    return pltpu.make_async_copy(src_hbm.at[pl.ds(src_row, 1)], dst_ref.at[pl.ds(r, 1)], sem)


def _gather_start(idx_hbm, chunk, src_hbm, dst_ref, idx_smem, sem_idx, sem_rows):
    n = dst_ref.shape[0]
    off = pl.multiple_of(chunk * n, n)
    cp = pltpu.make_async_copy(idx_hbm.at[pl.ds(off, n)], idx_smem, sem_idx)
    cp.start()
    cp.wait()

    def issue(blk, carry):
        base = pl.multiple_of(blk * GATHER_UNROLL, GATHER_UNROLL)
        for u in range(GATHER_UNROLL):
            _row_copy(src_hbm, idx_smem[base + u], dst_ref, base + u, sem_rows).start(priority=u % 2)
        return carry

    lax.fori_loop(0, n // GATHER_UNROLL, issue, 0)


def _gather_wait(src_hbm, dst_ref, sem_rows):
    def drain(r, carry):
        _row_copy(src_hbm, 0, dst_ref, r, sem_rows).wait()
        return carry

    lax.fori_loop(0, dst_ref.shape[0], drain, 0, unroll=GATHER_UNROLL)


def _moe_kernel(te_ref, nt_ref, idx_hbm, x_hbm, wg_ref, wu_ref, wd_ref, sw_ref, o_ref,
                xf_ref, xb_ref, acc_ref, idx_smem, sem_idx, sem_rows):
    i = pl.program_id(0)
    j = pl.program_id(1)
    n_used = nt_ref[0]

    @pl.when(i < n_used)
    def _():
        @pl.when(j == 0)
        def _():
            slot = i % 2
            @pl.when(i == 0)
            def _():
                _gather_start(idx_hbm, i, x_hbm, xf_ref.at[slot], idx_smem, sem_idx, sem_rows.at[slot])

            _gather_wait(x_hbm, xf_ref.at[slot], sem_rows.at[slot])

            @pl.when(i + 1 < n_used)
            def _():
                _gather_start(idx_hbm, i + 1, x_hbm, xf_ref.at[1 - slot], idx_smem, sem_idx, sem_rows.at[1 - slot])

            xb_ref[...] = xf_ref[slot].astype(BF16)
            acc_ref[...] = jnp.zeros_like(acc_ref)

        x = xb_ref[...]
        a = (_silu(_dot(x, wg_ref[0])) * _dot(x, wu_ref[0])).astype(BF16)
        acc_ref[...] += _dot(a, wd_ref[0])

    @pl.when(j == pl.num_programs(1) - 1)
    def _():
        o_ref[...] = acc_ref[...] * sw_ref[...]


def moe_experts(xn, sorted_tok, sorted_w, tile_expert, n_tiles_used, w_gate_up, w_down, *, tf=512):
    n, d = xn.shape
    ne, f, _ = w_down.shape
    tm = MOE_TILE
    p = sorted_tok.shape[0]
    n_tiles = p // tm
    tf = min(tf, f)
    nf = f // tf
    grid_spec = pltpu.PrefetchScalarGridSpec(
        num_scalar_prefetch=2,
        grid=(n_tiles, nf),
        in_specs=[pl.BlockSpec(memory_space=pl.ANY),
                  pl.BlockSpec(memory_space=pl.ANY),
                  pl.BlockSpec((1, d, tf), lambda i, j, te, nt: (te[i], 0, j)),
                  pl.BlockSpec((1, d, tf), lambda i, j, te, nt: (te[i], 0, j + nf)),
                  pl.BlockSpec((1, tf, d), lambda i, j, te, nt: (te[i], j, 0)),
                  pl.BlockSpec((tm, 1), lambda i, j, te, nt: (i, 0))],
        out_specs=pl.BlockSpec((tm, d), lambda i, j, te, nt: (i, 0)),
        scratch_shapes=[pltpu.VMEM((2, tm, d), F32), pltpu.VMEM((tm, d), BF16), pltpu.VMEM((tm, d), F32),
                        pltpu.SMEM((tm,), jnp.int32), pltpu.SemaphoreType.DMA, pltpu.SemaphoreType.DMA((2,))],
    )
    return pl.pallas_call(
        _moe_kernel,
        grid_spec=grid_spec,
        out_shape=jax.ShapeDtypeStruct((p, d), F32),
        compiler_params=_cparams(("arbitrary", "arbitrary"), 56),
        name="moe_experts",
    )(tile_expert, n_tiles_used, sorted_tok, xn, w_gate_up, w_gate_up, w_down, sorted_w.reshape(p, 1))


def _combine_kernel(idx_hbm, y_hbm, h_ref, nw_ref, o_ref, buf_ref, idx_smem, sem_idx, sem_rows, *, final_norm):
    i = pl.program_id(0)
    tc = h_ref.shape[0]
    slot = i % 2

    @pl.when(i == 0)
    def _():
        _gather_start(idx_hbm, i, y_hbm, buf_ref.at[slot], idx_smem, sem_idx, sem_rows.at[slot])

    _gather_wait(y_hbm, buf_ref.at[slot], sem_rows.at[slot])

    @pl.when(i + 1 < pl.num_programs(0))
    def _():
        _gather_start(idx_hbm, i + 1, y_hbm, buf_ref.at[1 - slot], idx_smem, sem_idx, sem_rows.at[1 - slot])

    out = h_ref[...] + buf_ref[slot, :tc, :] + buf_ref[slot, tc:, :]
    if final_norm:
        out = _rms(out, nw_ref[...])
    o_ref[...] = out


def moe_combine(h, y_sorted, pos_tiles, nw, *, final_norm):
    n, d = h.shape
    tc = MOE_TILE // TOP_K
    return pl.pallas_call(
        functools.partial(_combine_kernel, final_norm=final_norm),
        grid=(n // tc,),
        in_specs=[pl.BlockSpec(memory_space=pl.ANY),
                  pl.BlockSpec(memory_space=pl.ANY),
                  pl.BlockSpec((tc, d), lambda i: (i, 0)),
                  pl.BlockSpec((1, d), lambda i: (0, 0))],
        out_specs=pl.BlockSpec((tc, d), lambda i: (i, 0)),
        out_shape=jax.ShapeDtypeStruct((n, d), F32),
        scratch_shapes=[pltpu.VMEM((2, TOP_K * tc, d), F32), pltpu.SMEM((TOP_K * tc,), jnp.int32),
                        pltpu.SemaphoreType.DMA, pltpu.SemaphoreType.DMA((2,))],
        compiler_params=_cparams(("arbitrary",), 40),
        name="moe_combine",
    )(pos_tiles, y_sorted, h, nw.reshape(1, d))


def routed_swiglu(h, nw, w_router, w_gate_up, w_down, final_nw):
    n, d = h.shape
    tm = MOE_TILE
    tc = tm // TOP_K
    xn, meta, cnt = moe_router(h, nw, w_router)
    counts = cnt[0, :N_EXPERTS].astype(jnp.int32)
    tiles_per = (counts + tm - 1) // tm
    tile_end = jnp.cumsum(tiles_per)
    start = (tile_end - tiles_per) * tm
    e = meta[:, 0:2].astype(jnp.int32)
    pos = start[e] + meta[:, 2:4].astype(jnp.int32)
    p = TOP_K * n + N_EXPERTS * tm
    n_tiles = p // tm
    assign = jnp.full((p,), -1, jnp.int32).at[pos.reshape(-1)].set(
        jnp.arange(TOP_K * n, dtype=jnp.int32), unique_indices=True)
    valid = assign >= 0
    sorted_tok = jnp.where(valid, assign // TOP_K, 0)
    sorted_w = jnp.where(valid, meta[:, 4:6].reshape(-1)[jnp.maximum(assign, 0)], 0.0)
    tile_expert = jnp.minimum(jnp.searchsorted(tile_end, jnp.arange(n_tiles, dtype=jnp.int32), side="right"),
                              N_EXPERTS - 1).astype(jnp.int32)
    y = moe_experts(xn, sorted_tok, sorted_w, tile_expert, tile_end[-1:].astype(jnp.int32), w_gate_up, w_down)
    pos_tiles = pos.reshape(n // tc, tc, TOP_K).transpose(0, 2, 1).reshape(-1)
    return moe_combine(h, y, pos_tiles, final_nw if final_nw is not None else nw, final_norm=final_nw is not None)


def kernel(x, positions, rel_bias, attn_norm, ffn_norm, final_norm, gdn_w_in, gdn_conv_w, gdn_a_log, gdn_dt_bias,
           gdn_norm_w, gdn_w_out, sb_w_in, sb_w_out, swa_w_in, swa_sinks, swa_w_out, mla_w_in, mla_q_norm,
           mla_w_q_b, mla_kv_norm, mla_w_kv_b, mla_w_out, ffn_w_gate_up, ffn_w_down, moe_w_router, moe_w_gate_up,
           moe_w_down):
    batch, seq, d = x.shape
    n = batch * seq
    depth = attn_norm.shape[0]
    h = x.reshape(n, d).astype(F32)
    bf = lambda a: a.astype(BF16)
    for i in range(depth):
        mixer, occ = i % 4, i // 4
        if mixer == 0:
            w_in = gdn_w_in[occ]
            qkvz_dim = 4 * GDN_HEADS * GDN_HEAD_DIM
            qkvz = norm_matmul(h, attn_norm[i], bf(w_in[:, :qkvz_dim]), tn=1024)
            gcol, grow = gdn_gates(h, attn_norm[i], w_in[:, qkvz_dim:].astype(F32), gdn_a_log[occ], gdn_dt_bias[occ])
            o = gdn_core(qkvz, gdn_conv_w[occ].astype(F32), gcol, grow, gdn_norm_w[occ].astype(F32), batch, seq)
            h = matmul_residual(o, bf(gdn_w_out[occ]), h)
        elif mixer == 1:
            qkv = norm_matmul(h, attn_norm[i], bf(sb_w_in[occ]), tn=1024)
            o = sb_attention(qkv, batch, seq)
            h = matmul_residual(o, bf(sb_w_out[occ]), h)
        elif mixer == 2:
            qkv = norm_matmul(h, attn_norm[i], bf(swa_w_in[occ]), tn=256)
            o = swa_attention(qkv, swa_sinks[occ], rel_bias, batch, seq)
            h = matmul_residual(o, bf(swa_w_out[occ]), h)
        else:
            w_in = mla_w_in[occ]
            base = MLA_Q_RANK + MLA_KV_RANK
            kr = w_in[:, base:]
            zpad = jnp.zeros((d, LANES - MLA_ROPE), w_in.dtype)
            w_all = jnp.concatenate([w_in[:, :base], kr, zpad, _rotate_half_cols(kr), zpad], axis=1)
            c = norm_matmul(h, attn_norm[i], bf(w_all), tn=w_all.shape[1], out_dtype=F32)
            qcat, kcat, v = mla_qkv(c, positions, mla_q_norm[occ], mla_w_q_b[occ], mla_kv_norm[occ], mla_w_kv_b[occ])
            o = mla_attention(qcat, kcat, v, batch, seq)
            h = matmul_residual(o, bf(mla_w_out[occ]), h)
        f = i // 2
        last = i == depth - 1
        if i % 2 == 0:
            h = ffn_dense(h, ffn_norm[i], bf(ffn_w_gate_up[f]), bf(ffn_w_down[f]))
            if last:
                h = final_rmsnorm(h, final_norm)
        else:
            h = routed_swiglu(h, ffn_norm[i], moe_w_router[f], bf(moe_w_gate_up[f]), bf(moe_w_down[f]),
                              final_norm if last else None)
    return h.reshape(batch, seq, d).astype(x.dtype)


def _final_norm_kernel(x_ref, nw_ref, o_ref):
    o_ref[...] = _rms(x_ref[...], nw_ref[...])


def final_rmsnorm(h, nw, *, tm=512):
    m, d = h.shape
    tm = min(tm, m)
    return pl.pallas_call(
        _final_norm_kernel,
        grid=(m // tm,),
        in_specs=[pl.BlockSpec((tm, d), lambda i: (i, 0)), pl.BlockSpec((1, d), lambda i: (0, 0))],
        out_specs=pl.BlockSpec((tm, d), lambda i: (i, 0)),
        out_shape=jax.ShapeDtypeStruct((m, d), F32),
        compiler_params=_cparams(("parallel",), 40),
        name="final_rmsnorm",
    )(h, nw.reshape(1, d))
```
